```python
import math
import jax, jax.numpy as jnp
from jax import lax
import numpy as np

D_MODEL = 2048
BATCH = 2
SEQ = 16384
DEPTH = 1
DEC_BATCH = 16
DEC_SEQ = 64
PAST_LEN = 4096

CHUNK = 64
GDN_DK = 128
GDN_DV = 128
GDN_HEADS = (D_MODEL // 2) // GDN_DV
HG_F = 128
HG_V = 128
HG_HEADS = (D_MODEL // 2) // HG_V
CONV_W = 4
QK_WIDTH = GDN_HEADS * GDN_DK
GDN_WIDTH = GDN_HEADS * GDN_DV
QKV_WIDTH = 2 * QK_WIDTH + GDN_WIDTH
HG_F_WIDTH = HG_HEADS * HG_F
HG_WIDTH = HG_HEADS * HG_V
MIX_WIDTH = GDN_WIDTH + HG_WIDTH
PROJ_SIZES = (QKV_WIDTH, GDN_WIDTH, GDN_HEADS, GDN_HEADS, HG_F_WIDTH, HG_F_WIDTH, HG_WIDTH, HG_WIDTH)
PROJ_WIDTH = QKV_WIDTH + GDN_WIDTH + 2 * GDN_HEADS + 2 * HG_F_WIDTH + 2 * HG_WIDTH
N_EXPERTS = 32
TOP_K = 4
D_FF = D_MODEL
SWIGLU_LIMIT = 7.0
SWIGLU_ALPHA = 1.702
MOE_BLOCK = 256
NORM_EPS = 1e-6

kernel_name = 'hybrid_gdn_hgrn2_moe_stream'


def _rms_norm(x, g):
    x32 = x.astype(jnp.float32)
    y = x32 * lax.rsqrt(jnp.mean(x32 * x32, axis=-1, keepdims=True) + NORM_EPS)
    return (y * g.astype(jnp.float32)).astype(x.dtype)


def _gated_head_norm(o, g, z):
    o = o * lax.rsqrt(jnp.mean(o * o, axis=-1, keepdims=True) + NORM_EPS)
    return o * g.astype(jnp.float32) * jax.nn.silu(z.astype(jnp.float32))


def _l2norm(x):
    return x * lax.rsqrt(jnp.sum(x * x, axis=-1, keepdims=True) + 1e-6)


def _block_len(T):
    return CHUNK if T % CHUNK == 0 else T


def _to_blocks(a, L):
    B, T = a.shape[:2]
    a = a.reshape((B, T // L, L) + a.shape[2:])
    return jnp.moveaxis(a, (1, 3), (0, 2))


def _from_blocks(o):
    o = jnp.moveaxis(o, (0, 2), (1, 3))
    B, N, L, H, d = o.shape
    return o.reshape(B, N * L, H, d)


def _gated_delta_rule(q, k, v, beta, g, s0):
    L = _block_len(q.shape[1])
    q, k, v, beta, g = (_to_blocks(a, L) for a in (q, k, v, beta, g))
    gc = jnp.cumsum(g, axis=-1)
    diff = gc[..., :, None] - gc[..., None, :]
    incl = jnp.tril(jnp.ones((L, L), bool))
    strict = jnp.tril(jnp.ones((L, L), bool), -1)
    dec_incl = jnp.exp(jnp.where(incl, diff, -jnp.inf))
    dec_strict = jnp.where(strict, dec_incl, 0.0)
    a_mat = beta[..., :, None] * jnp.einsum('nbhrk,nbhjk->nbhrj', k, k) * dec_strict
    rhs = jnp.concatenate([beta[..., None] * v, (beta * jnp.exp(gc))[..., None] * k], axis=-1)
    sol = lax.linalg.triangular_solve(jnp.eye(L, dtype=jnp.float32) + a_mat, rhs,
                                      left_side=True, lower=True, unit_diagonal=True)
    dv = v.shape[-1]
    u, w = sol[..., :dv], sol[..., dv:]
    attn = jnp.einsum('nbhrk,nbhjk->nbhrj', q, k) * dec_incl
    q_dec = q * jnp.exp(gc)[..., None]
    k_dec = k * jnp.exp(gc[..., -1:] - gc)[..., None]
    g_tot = jnp.exp(gc[..., -1])

    def step(S, xs):
        q_b, a_b, u_b, w_b, k_b, g_b = xs
        delta = u_b - jnp.einsum('bhrk,bhkv->bhrv', w_b, S)
        o = jnp.einsum('bhrk,bhkv->bhrv', q_b, S) + jnp.einsum('bhrj,bhjv->bhrv', a_b, delta)
        S = g_b[..., None, None] * S + jnp.einsum('bhjk,bhjv->bhkv', k_b, delta)
        return S, o

    s_final, o = lax.scan(step, s0, (q_dec, attn, u, w, k_dec, g_tot))
    return _from_blocks(o), s_final


def _hgrn2_recurrence(q, k, v, log_f, s0):
    L = _block_len(q.shape[1])
    q, k, v, log_f = (_to_blocks(a, L) for a in (q, k, v, log_f))
    bc = jnp.cumsum(log_f, axis=-2)
    q_dec = q * jnp.exp(bc)
    k_dec = k * jnp.exp(bc[..., -1:, :] - bc)
    f_tot = jnp.exp(bc[..., -1, :])
    incl = jnp.tril(jnp.ones((L, L), bool))[:, :, None]

    def step(S, xs):
        q_b, k_b, v_b, bc_b, qd_b, kd_b, ft_b = xs
        diff = bc_b[..., :, None, :] - bc_b[..., None, :, :]
        dec = jnp.exp(jnp.where(incl, diff, -jnp.inf))
        attn = jnp.einsum('bhrf,bhjf,bhrjf->bhrj', q_b, k_b, dec)
        o = jnp.einsum('bhrf,bhfv->bhrv', qd_b, S) + jnp.einsum('bhrj,bhjv->bhrv', attn, v_b)
        S = ft_b[..., :, None] * S + jnp.einsum('bhjf,bhjv->bhfv', kd_b, v_b)
        return S, o

    s_final, o = lax.scan(step, s0, (q, k, v, bc, q_dec, k_dec, f_tot))
    return _from_blocks(o), s_final


def _mixer(h, conv_buf, s_gdn, s_hgrn, w_in, conv_w, A_log, dt_bias, gdn_norm_g, lb, hg_norm_g, w_out):
    B, T, _ = h.shape
    f32 = jnp.float32
    proj = h @ w_in
    split_idx = np.cumsum(PROJ_SIZES)[:-1].tolist()
    qkv, a_gate, a_beta, a_decay, hq, hf, hi, h_gate = jnp.split(proj, split_idx, axis=-1)

    conv_in = jnp.concatenate([conv_buf.astype(qkv.dtype), qkv], axis=1)
    c32 = conv_in.astype(f32)
    w32 = conv_w.astype(f32)
    conv_out = c32[:, 0:T] * w32[0]
    for j in range(1, CONV_W):
        conv_out = conv_out + c32[:, j:j + T] * w32[j]
    qkv_c = jax.nn.silu(conv_out)
    q_a, k_a, v_a = jnp.split(qkv_c, [QK_WIDTH, 2 * QK_WIDTH], axis=-1)
    q_a = _l2norm(q_a.reshape(B, T, GDN_HEADS, GDN_DK)) * (GDN_DK ** -0.5)
    k_a = _l2norm(k_a.reshape(B, T, GDN_HEADS, GDN_DK))
    v_a = v_a.reshape(B, T, GDN_HEADS, GDN_DV)
    beta = jax.nn.sigmoid(a_beta.astype(f32))
    g = -jnp.exp(A_log.astype(f32)) * jax.nn.softplus(a_decay.astype(f32) + dt_bias.astype(f32))
    o_a, s_gdn_new = _gated_delta_rule(q_a, k_a, v_a, beta, g, s_gdn.astype(f32))
    o_a = _gated_head_norm(o_a, gdn_norm_g, a_gate.reshape(B, T, GDN_HEADS, GDN_DV))

    q_b = jax.nn.silu(hq.astype(f32)).reshape(B, T, HG_HEADS, HG_F)
    f = (lb + (1.0 - lb) * jax.nn.sigmoid(hf.astype(f32))).reshape(B, T, HG_HEADS, HG_F)
    v_b = hi.astype(f32).reshape(B, T, HG_HEADS, HG_V)
    o_b, s_hg_new = _hgrn2_recurrence(q_b, 1.0 - f, v_b, jnp.log(f), s_hgrn.astype(f32))
    o_b = _gated_head_norm(o_b, hg_norm_g, h_gate.reshape(B, T, HG_HEADS, HG_V))

    mixed = jnp.concatenate([o_a.reshape(B, T, GDN_WIDTH), o_b.reshape(B, T, HG_WIDTH)], axis=-1)
    out = mixed.astype(h.dtype) @ w_out
    new_buf = conv_in[:, conv_in.shape[1] - (CONV_W - 1):]
    return (out, new_buf.astype(conv_buf.dtype), s_gdn_new.astype(s_gdn.dtype),
            s_hg_new.astype(s_hgrn.dtype))


def _moe_ffn(h, w_router, b_router, w_gate_up, b_gate_up, w_down, b_down):
    B, T, D = h.shape
    n_tok = B * T
    n_assign = n_tok * TOP_K
    n_blocks = -(-(n_assign + N_EXPERTS * (MOE_BLOCK - 1)) // MOE_BLOCK)
    xt = h.reshape(n_tok, D)
    logits = (xt @ w_router).astype(jnp.float32) + b_router.astype(jnp.float32)
    top_val, top_idx = lax.top_k(logits, TOP_K)
    gates = jax.nn.softmax(top_val, axis=-1)
    e_flat = top_idx.reshape(-1)
    order = jnp.argsort(e_flat, stable=True)
    e_s = e_flat[order]
    tok_s = (order // TOP_K).astype(jnp.int32)
    gate_s = gates.reshape(-1)[order]
    counts = jax.ops.segment_sum(jnp.ones_like(e_flat), e_flat, num_segments=N_EXPERTS)
    padded = (counts + MOE_BLOCK - 1) // MOE_BLOCK * MOE_BLOCK
    pad_end = jnp.cumsum(padded)
    pad_start = pad_end - padded
    start = jnp.cumsum(counts) - counts
    dest = pad_start[e_s] + jnp.arange(n_assign, dtype=jnp.int32) - start[e_s]
    n_rows = n_blocks * MOE_BLOCK
    row_tok = jnp.full((n_rows,), n_tok, jnp.int32).at[dest].set(tok_s)
    row_gate = jnp.zeros((n_rows,), jnp.float32).at[dest].set(gate_s)
    block_exp = jnp.minimum(
        jnp.searchsorted(pad_end, jnp.arange(n_blocks, dtype=jnp.int32) * MOE_BLOCK, side='right'),
        N_EXPERTS - 1)
    x_pad = jnp.concatenate([xt, jnp.zeros((1, D), xt.dtype)], axis=0)

    def expert_block(y, inp):
        toks, gts, e = inp
        xb = x_pad[toks]
        gu = (xb @ w_gate_up[e]).astype(jnp.float32) + b_gate_up[e].astype(jnp.float32)
        gate = jnp.minimum(gu[:, :D_FF], SWIGLU_LIMIT)
        up = jnp.clip(gu[:, D_FF:], -SWIGLU_LIMIT, SWIGLU_LIMIT)
        act = ((up + 1.0) * gate * jax.nn.sigmoid(SWIGLU_ALPHA * gate)).astype(xb.dtype)
        out = (act @ w_down[e]).astype(jnp.float32) + b_down[e].astype(jnp.float32)
        return y.at[toks].add(out * gts[:, None]), None

    y0 = jnp.zeros((n_tok + 1, D), jnp.float32)
    y, _ = lax.scan(expert_block, y0,
                    (row_tok.reshape(n_blocks, MOE_BLOCK), row_gate.reshape(n_blocks, MOE_BLOCK), block_exp))
    return y[:n_tok].reshape(B, T, D).astype(h.dtype)


def _trunk(x, conv_buf, s_gdn, s_hgrn, weights):
    (ln1_g, w_in, gdn_conv_w, gdn_A_log, gdn_dt_bias, gdn_norm_g, hgrn_lb_logits, hgrn_norm_g,
     w_out, ln2_g, w_router, b_router, w_gate_up, b_gate_up, w_down, b_down, ln_f_g) = weights
    lower_bounds = jnp.cumsum(jax.nn.softmax(hgrn_lb_logits.astype(jnp.float32), axis=0), axis=0)
    new_gdn, new_conv, new_hg = [], [], []
    for l in range(DEPTH):
        mix, cb, sg, sh = _mixer(_rms_norm(x, ln1_g[l]), conv_buf[l], s_gdn[l], s_hgrn[l], w_in[l],
                                 gdn_conv_w[l], gdn_A_log[l], gdn_dt_bias[l], gdn_norm_g[l],
                                 lower_bounds[l], hgrn_norm_g[l], w_out[l])
        x = x + mix
        x = x + _moe_ffn(_rms_norm(x, ln2_g[l]), w_router[l], b_router[l], w_gate_up[l],
                         b_gate_up[l], w_down[l], b_down[l])
        new_gdn.append(sg)
        new_conv.append(cb)
        new_hg.append(sh)
    return _rms_norm(x, ln_f_g), jnp.stack(new_gdn), jnp.stack(new_conv), jnp.stack(new_hg)


def setup_inputs(seed: int = 0) -> dict:
    key = jax.random.key(seed)
    ks = jax.random.split(key, 24)
    f32 = jnp.float32

    def nrm(k, shape, scale):
        return jax.random.normal(k, shape, f32) * scale

    dt = jnp.exp(jax.random.uniform(ks[9], (DEPTH, GDN_HEADS), f32, math.log(1e-3), math.log(1e-1)))
    return {
        'x_prompt': nrm(ks[0], (BATCH, SEQ, D_MODEL), 1.0),
        'x_sample': nrm(ks[1], (DEC_BATCH, DEC_SEQ, D_MODEL), 1.0),
        'state_gdn': nrm(ks[2], (DEPTH, DEC_BATCH, GDN_HEADS, GDN_DK, GDN_DV), 0.05),
        'cache_gdn_conv': nrm(ks[3], (DEPTH, DEC_BATCH, CONV_W - 1, QKV_WIDTH), 1.0),
        'state_hgrn': nrm(ks[4], (DEPTH, DEC_BATCH, HG_HEADS, HG_F, HG_V), 0.1),
        'ln1_g': 1.0 + nrm(ks[5], (DEPTH, D_MODEL), 0.02),
        'w_in': nrm(ks[6], (DEPTH, D_MODEL, PROJ_WIDTH), D_MODEL ** -0.5),
        'gdn_conv_w': nrm(ks[7], (DEPTH, CONV_W, QKV_WIDTH), CONV_W ** -0.5),
        'gdn_A_log': jnp.log(jax.random.uniform(ks[8], (DEPTH, GDN_HEADS), f32, 1.0, 16.0)),
        'gdn_dt_bias': dt + jnp.log(-jnp.expm1(-dt)),
        'gdn_norm_g': 1.0 + nrm(ks[10], (DEPTH, GDN_DV), 0.02),
        'hgrn_lb_logits': nrm(ks[11], (DEPTH + 1, HG_F_WIDTH), 0.5),
        'hgrn_norm_g': 1.0 + nrm(ks[12], (DEPTH, HG_V), 0.02),
        'w_out': nrm(ks[13], (DEPTH, MIX_WIDTH, D_MODEL), MIX_WIDTH ** -0.5),
        'ln2_g': 1.0 + nrm(ks[14], (DEPTH, D_MODEL), 0.02),
        'w_router': nrm(ks[15], (DEPTH, D_MODEL, N_EXPERTS), D_MODEL ** -0.5),
        'b_router': nrm(ks[16], (DEPTH, N_EXPERTS), 0.01),
        'w_gate_up': nrm(ks[17], (DEPTH, N_EXPERTS, D_MODEL, 2 * D_FF), D_MODEL ** -0.5),
        'b_gate_up': nrm(ks[18], (DEPTH, N_EXPERTS, 2 * D_FF), 0.01),
        'w_down': nrm(ks[19], (DEPTH, N_EXPERTS, D_FF, D_MODEL), D_FF ** -0.5),
        'b_down': nrm(ks[20], (DEPTH, N_EXPERTS, D_MODEL), 0.01),
        'ln_f_g': 1.0 + nrm(ks[21], (D_MODEL,), 0.02),
    }


def reference(x_prompt, x_sample, state_gdn, cache_gdn_conv, state_hgrn, ln1_g, w_in, gdn_conv_w,
              gdn_A_log, gdn_dt_bias, gdn_norm_g, hgrn_lb_logits, hgrn_norm_g, w_out, ln2_g,
              w_router, b_router, w_gate_up, b_gate_up, w_down, b_down, ln_f_g):
    weights = (ln1_g, w_in, gdn_conv_w, gdn_A_log, gdn_dt_bias, gdn_norm_g, hgrn_lb_logits,
               hgrn_norm_g, w_out, ln2_g, w_router, b_router, w_gate_up, b_gate_up, w_down,
               b_down, ln_f_g)
    nb = x_prompt.shape[0]
    dt = x_prompt.dtype
    conv0 = jnp.zeros((DEPTH, nb, CONV_W - 1, QKV_WIDTH), dt)
    gdn0 = jnp.zeros((DEPTH, nb, GDN_HEADS, GDN_DK, GDN_DV), dt)
    hg0 = jnp.zeros((DEPTH, nb, HG_HEADS, HG_F, HG_V), dt)
    y_prompt, gdn_p, conv_p, hg_p = _trunk(x_prompt, conv0, gdn0, hg0, weights)
    y_sample, gdn_s, conv_s, hg_s = _trunk(x_sample, cache_gdn_conv, state_gdn, state_hgrn, weights)
    return (y_prompt, y_sample, gdn_p, conv_p, hg_p, gdn_s, conv_s, hg_s)
```

```python
import functools

import jax
import jax.numpy as jnp
from jax import lax
from jax.experimental import pallas as pl
from jax.experimental.pallas import tpu as pltpu

F32 = jnp.float32
BF16 = jnp.bfloat16
U32 = jnp.uint32
I32 = jnp.int32

CHUNK = 64
CONV_W = 4
TOP_K = 4
NORM_EPS = 1e-6
L2_EPS = 1e-6
SWIGLU_LIMIT = 7.0
SWIGLU_ALPHA = 1.702
HEAD_DIM = 128
LANES = 128
HGRN_SUB = 16
VMEM_LIMIT = 56 * 1024 * 1024


def _pick(n, cands):
    for c in cands:
        if n % c == 0:
            return c
    raise ValueError(f"no tile in {cands} divides {n}")


def _cparams(sem):
    return pltpu.CompilerParams(dimension_semantics=sem, vmem_limit_bytes=VMEM_LIMIT)


def _sigmoid(x):
    return 1.0 / (1.0 + jnp.exp(-x))


def _silu(x):
    return x * _sigmoid(x)


def _softplus(x):
    return jnp.maximum(x, 0.0) + jnp.log1p(jnp.exp(-jnp.abs(x)))


def _rms(x, g):
    return x * lax.rsqrt(jnp.mean(x * x, axis=-1, keepdims=True) + NORM_EPS) * g


def _dot(a, b):
    return jnp.dot(a.astype(BF16), b.astype(BF16), preferred_element_type=F32)


def _dot_nt(a, b):
    return lax.dot_general(a.astype(BF16), b.astype(BF16), (((1,), (1,)), ((), ())),
                           preferred_element_type=F32)


def _dot_tn(a, b):
    return lax.dot_general(a.astype(BF16), b.astype(BF16), (((0,), (0,)), ((), ())),
                           preferred_element_type=F32)


def _split3(x):
    p1 = x.astype(BF16)
    r1 = x - p1.astype(F32)
    p2 = r1.astype(BF16)
    p3 = (r1 - p2.astype(F32)).astype(BF16)
    return p1, p2, p3


def _dot_exact_lhs(m_bf16, x):
    p1, p2, p3 = _split3(x)
    d = lambda p: jnp.dot(m_bf16, p, preferred_element_type=F32)
    return d(p1) + d(p2) + d(p3)


def _inproj_kernel(x_ref, g_ref, wm_ref, ws_ref, pm_ref, ps_ref, h_scr):
    @pl.when(pl.program_id(1) == 0)
    def _():
        hb = _rms(x_ref[...], g_ref[...]).astype(BF16)
        h_scr[...] = hb
        ps_ref[...] = jnp.dot(hb, ws_ref[...], preferred_element_type=F32)

    pm_ref[...] = jnp.dot(h_scr[...], wm_ref[...], preferred_element_type=F32)


def _inproj(x, g, w_main, w_small):
    t, d = x.shape
    nm = w_main.shape[1]
    tm = _pick(t, (1024, 512, 256, 128, 64))
    tn = _pick(nm, (1024, 512, 256, 128))
    return pl.pallas_call(
        _inproj_kernel,
        grid=(t // tm, nm // tn),
        in_specs=[
            pl.BlockSpec((tm, d), lambda i, j: (i, 0)),
            pl.BlockSpec((1, d), lambda i, j: (0, 0)),
            pl.BlockSpec((d, tn), lambda i, j: (0, j)),
            pl.BlockSpec((d, LANES), lambda i, j: (0, 0)),
        ],
        out_specs=[
            pl.BlockSpec((tm, tn), lambda i, j: (i, j)),
            pl.BlockSpec((tm, LANES), lambda i, j: (i, 0)),
        ],
        out_shape=[jax.ShapeDtypeStruct((t, nm), F32), jax.ShapeDtypeStruct((t, LANES), F32)],
        scratch_shapes=[pltpu.VMEM((tm, d), BF16)],
        compiler_params=_cparams(("arbitrary", "arbitrary")),
        name="inproj",
    )(x, g, w_main, w_small)


def _unit_lower_inverse(a, lvl_masks, eye):
    x = eye - jnp.where(lvl_masks[0], a, 0.0)
    for m in lvl_masks[1:]:
        l = jnp.where(m, a, 0.0)
        x = x - _dot(_dot(x, l), x)
    return x


def _mixer_kernel(heads, n_prompt_chunks, n_prompt_total, n_sample_chunks,
                  pm_ref, ps_ref, cw_ref, ad_ref, gng_ref, lb_ref, hng_ref,
                  sg0_ref, sh0_ref, cv0_ref,
                  mix_ref, sg_ref, sh_ref, cv_ref, cbuf, qkv_scr):
    c = pl.program_id(0)
    w = heads * HEAD_DIM
    L = CHUNK
    is_start = jnp.where(c < n_prompt_total, (c % n_prompt_chunks) == 0,
                         ((c - n_prompt_total) % n_sample_chunks) == 0)

    @pl.when(is_start)
    def _():
        sg_ref[...] = sg0_ref[...]
        sh_ref[...] = sh0_ref[...]
        cbuf[0:8, :] = cv0_ref[0]

    cbuf[8:8 + L, :] = pm_ref[:, 0:3 * w]
    conv = cbuf[5:5 + L, :] * cw_ref[0:1, :]
    for j in range(1, CONV_W):
        conv = conv + cbuf[5 + j:5 + j + L, :] * cw_ref[j:j + 1, :]
    tail = cbuf[L:L + 8, :]
    cv_ref[0] = tail
    cbuf[0:8, :] = tail
    qkv_scr[...] = _silu(conv)

    row = lax.broadcasted_iota(I32, (L, L), 0)
    col = lax.broadcasted_iota(I32, (L, L), 1)
    incl = row >= col
    strict = row > col
    eye = (row == col).astype(F32)
    tri_incl = incl.astype(BF16)
    lvl_masks = []
    for lg2 in range(L.bit_length() - 1):
        same_pair = (row >> (lg2 + 1)) == (col >> (lg2 + 1))
        lower_left = (((row >> lg2) & 1) == 1) & (((col >> lg2) & 1) == 0)
        lvl_masks.append(same_pair & lower_left)

    ps = ps_ref[...]
    beta_all = _sigmoid(ps)
    g_all = -jnp.exp(ad_ref[0:1, :]) * _softplus(ps + ad_ref[1:2, :])
    gc_all = _dot_exact_lhs(tri_incl, g_all)
    gc_t = gc_all.T

    for h in range(heads):
        sl = slice(h * HEAD_DIM, (h + 1) * HEAD_DIM)
        q_c = qkv_scr[:, h * HEAD_DIM:(h + 1) * HEAD_DIM]
        k_c = qkv_scr[:, w + h * HEAD_DIM:w + (h + 1) * HEAD_DIM]
        v = qkv_scr[:, 2 * w + h * HEAD_DIM:2 * w + (h + 1) * HEAD_DIM]
        q = q_c * lax.rsqrt(jnp.sum(q_c * q_c, axis=-1, keepdims=True) + L2_EPS) * (HEAD_DIM ** -0.5)
        k = k_c * lax.rsqrt(jnp.sum(k_c * k_c, axis=-1, keepdims=True) + L2_EPS)
        bcol = beta_all[:, h:h + 1]
        gcol = gc_all[:, heads + h:heads + h + 1]
        grow = gc_t[heads + h:heads + h + 1, :]
        glast = gc_all[L - 1:L, heads + h:heads + h + 1]
        dec_incl = jnp.where(incl, jnp.exp(jnp.where(incl, gcol - grow, 0.0)), 0.0)
        dec_strict = jnp.where(strict, dec_incl, 0.0)
        egc = jnp.exp(gcol)
        a_mat = bcol * _dot_nt(k, k) * dec_strict
        rhs = jnp.concatenate([bcol * v, (bcol * egc) * k], axis=-1)
        sol = _dot(_unit_lower_inverse(a_mat, lvl_masks, eye), rhs)
        u = sol[:, :HEAD_DIM]
        wy = sol[:, HEAD_DIM:]
        attn = _dot_nt(q, k) * dec_incl
        s = sg_ref[0, h]
        delta = u - _dot(wy, s)
        o = _dot(q * egc, s) + _dot(attn, delta)
        sg_ref[0, h] = jnp.exp(glast) * s + _dot_tn(k * jnp.exp(glast - gcol), delta)
        z = pm_ref[:, 3 * w + h * HEAD_DIM:3 * w + (h + 1) * HEAD_DIM]
        o = o * lax.rsqrt(jnp.mean(o * o, axis=-1, keepdims=True) + NORM_EPS) * gng_ref[...] * _silu(z)
        mix_ref[:, sl] = o.astype(mix_ref.dtype)

    nsub = L // HGRN_SUB
    sub_row = lax.broadcasted_iota(I32, (HGRN_SUB, L), 0)
    sub_col = lax.broadcasted_iota(I32, (HGRN_SUB, L), 1)
    sub_rows = lax.broadcasted_iota(I32, (L, HEAD_DIM), 0)
    for h in range(heads):
        o0 = 4 * w + h * HEAD_DIM
        qh = _silu(pm_ref[:, o0:o0 + HEAD_DIM])
        lbh = lb_ref[:, h * HEAD_DIM:(h + 1) * HEAD_DIM]
        f = lbh + (1.0 - lbh) * _sigmoid(pm_ref[:, o0 + w:o0 + w + HEAD_DIM])
        kh = 1.0 - f
        v = pm_ref[:, o0 + 2 * w:o0 + 2 * w + HEAD_DIM]
        bc = _dot_exact_lhs(tri_incl, jnp.log(f))
        a_parts = []
        for i in range(nsub):
            r0 = i * HGRN_SUB
            r1 = r0 + HGRN_SUB
            e_i = bc[r0 - 1:r0, :] if i > 0 else jnp.zeros((1, HEAD_DIM), F32)
            qt = qh[r0:r1] * jnp.exp(bc[r0:r1] - e_i)
            kt = kh * jnp.exp(jnp.where(sub_rows < r1, e_i - bc, 0.0))
            a_parts.append(jnp.where(sub_col <= sub_row + r0, _dot_nt(qt, kt), 0.0))
        o_intra = _dot(jnp.concatenate(a_parts, axis=0), v)
        blast = bc[L - 1:L, :]
        s = sh_ref[0, h]
        o = _dot(qh * jnp.exp(bc), s) + o_intra
        ft_col = jnp.exp(bc.T[:, L - 1:L])
        sh_ref[0, h] = ft_col * s + _dot_tn(kh * jnp.exp(blast - bc), v)
        z = pm_ref[:, o0 + 3 * w:o0 + 3 * w + HEAD_DIM]
        o = o * lax.rsqrt(jnp.mean(o * o, axis=-1, keepdims=True) + NORM_EPS) * hng_ref[...] * _silu(z)
        mix_ref[:, w + h * HEAD_DIM:w + (h + 1) * HEAD_DIM] = o.astype(mix_ref.dtype)


def _mixer(pm, ps, conv_w, ad, gng, lb, hng, sg0, sh0, cv0, *, heads, n_prompt_seq,
           n_prompt_chunks, n_sample_chunks):
    t = pm.shape[0]
    nm = pm.shape[1]
    w = heads * HEAD_DIM
    n_seq = sg0.shape[0]
    n_chunks = t // CHUNK
    n_prompt_total = n_prompt_seq * n_prompt_chunks

    def seq_of(c):
        return jnp.where(c < n_prompt_total, c // n_prompt_chunks,
                         n_prompt_seq + (c - n_prompt_total) // n_sample_chunks)

    const = lambda shape: pl.BlockSpec(shape, lambda c: (0,) * len(shape))
    st_spec = pl.BlockSpec((1, heads, HEAD_DIM, HEAD_DIM), lambda c: (seq_of(c), 0, 0, 0))
    cv_spec = pl.BlockSpec((1, 8, 3 * w), lambda c: (seq_of(c), 0, 0))
    kern = functools.partial(_mixer_kernel, heads, n_prompt_chunks, n_prompt_total, n_sample_chunks)
    return pl.pallas_call(
        kern,
        grid=(n_chunks,),
        in_specs=[
            pl.BlockSpec((CHUNK, nm), lambda c: (c, 0)),
            pl.BlockSpec((CHUNK, LANES), lambda c: (c, 0)),
            const((8, 3 * w)), const((8, LANES)), const((1, HEAD_DIM)), const((1, w)),
            const((1, HEAD_DIM)),
            st_spec, st_spec, cv_spec,
        ],
        out_specs=[
            pl.BlockSpec((CHUNK, 2 * w), lambda c: (c, 0)),
            st_spec, st_spec, cv_spec,
        ],
        out_shape=[
            jax.ShapeDtypeStruct((t, 2 * w), BF16),
            jax.ShapeDtypeStruct((n_seq, heads, HEAD_DIM, HEAD_DIM), F32),
            jax.ShapeDtypeStruct((n_seq, heads, HEAD_DIM, HEAD_DIM), F32),
            jax.ShapeDtypeStruct((n_seq, 8, 3 * w), F32),
        ],
        scratch_shapes=[pltpu.VMEM((CHUNK + 8, 3 * w), F32), pltpu.VMEM((CHUNK, 3 * w), F32)],
        compiler_params=_cparams(("arbitrary",)),
        name="mixer",
    )(pm, ps, conv_w, ad, gng, lb, hng, sg0, sh0, cv0)


def _outproj_kernel(x_ref, m_ref, wo_ref, g_ref, wr_ref, br_ref, x1_ref, hp_ref, lg_ref):
    x1 = x_ref[...] + jnp.dot(m_ref[...], wo_ref[...], preferred_element_type=F32)
    x1_ref[...] = x1
    h2 = _rms(x1, g_ref[...])
    half = h2.shape[1] // 2
    bits = lax.bitcast_convert_type(h2.astype(BF16).astype(F32), U32)
    hp_ref[...] = (bits[:, :half] >> 16) | (bits[:, half:] & jnp.uint32(0xFFFF0000))
    wr = wr_ref[...]
    wr_hi = wr.astype(BF16)
    wr_lo = (wr - wr_hi.astype(F32)).astype(BF16)
    h_hi = h2.astype(BF16)
    h_lo = (h2 - h_hi.astype(F32)).astype(BF16)
    nt = lambda a, b: lax.dot_general(a, b, (((1,), (1,)), ((), ())), preferred_element_type=F32)
    lg_ref[...] = nt(wr_hi, h_hi) + nt(wr_hi, h_lo) + nt(wr_lo, h_hi) + br_ref[...]


def _outproj(x, mixed, w_out, g2, w_router_t, b_router_col):
    t, d = x.shape
    e = w_router_t.shape[0]
    tm = _pick(t, (512, 256, 128))
    return pl.pallas_call(
        _outproj_kernel,
        grid=(t // tm,),
        in_specs=[
            pl.BlockSpec((tm, d), lambda i: (i, 0)),
            pl.BlockSpec((tm, d), lambda i: (i, 0)),
            pl.BlockSpec((d, d), lambda i: (0, 0)),
            pl.BlockSpec((1, d), lambda i: (0, 0)),
            pl.BlockSpec((e, d), lambda i: (0, 0)),
            pl.BlockSpec((e, 1), lambda i: (0, 0)),
        ],
        out_specs=[
            pl.BlockSpec((tm, d), lambda i: (i, 0)),
            pl.BlockSpec((tm, d // 2), lambda i: (i, 0)),
            pl.BlockSpec((e, tm), lambda i: (0, i)),
        ],
        out_shape=[
            jax.ShapeDtypeStruct((t, d), F32),
            jax.ShapeDtypeStruct((t, d // 2), U32),
            jax.ShapeDtypeStruct((e, t), F32),
        ],
        compiler_params=_cparams(("arbitrary",)),
        name="outproj",
    )(x, mixed, w_out, g2, w_router_t, b_router_col)


def _route_kernel(lg_ref, tri_ref, e_ref, gate_ref, rank_ref, cnt_ref, carry):
    @pl.when(pl.program_id(0) == 0)
    def _():
        carry[...] = jnp.zeros_like(carry)

    lg = lg_ref[...]
    n_exp, tb = lg.shape
    eio = lax.broadcasted_iota(I32, (n_exp, tb), 0).astype(F32)
    work = lg
    vals, onehots = [], []
    for k in range(TOP_K):
        m = jnp.max(work, axis=0, keepdims=True)
        idx = jnp.min(jnp.where(work == m, eio, float(n_exp)), axis=0, keepdims=True)
        oh = eio == idx
        e_ref[k:k + 1, :] = idx.astype(I32)
        vals.append(m)
        onehots.append(oh)
        work = jnp.where(oh, -jnp.inf, work)
    exps = [jnp.exp(v - vals[0]) for v in vals]
    denom = exps[0] + exps[1] + exps[2] + exps[3]
    for k in range(TOP_K):
        gate_ref[k:k + 1, :] = exps[k] / denom
    self32 = sum(oh.astype(F32) for oh in onehots)
    before = carry[...] + jnp.dot(self32.astype(BF16), tri_ref[...], preferred_element_type=F32)
    for k in range(TOP_K):
        rank_ref[k:k + 1, :] = jnp.sum(jnp.where(onehots[k], before, 0.0), axis=0,
                                       keepdims=True).astype(I32)
    pad = jnp.zeros((8 - TOP_K, tb), I32)
    e_ref[TOP_K:8, :] = pad
    rank_ref[TOP_K:8, :] = pad
    gate_ref[TOP_K:8, :] = pad.astype(F32)
    carry[...] = carry[...] + jnp.sum(self32, axis=1, keepdims=True)
    cnt_ref[...] = jnp.broadcast_to(carry[...], cnt_ref.shape)


def _route(logits_t):
    n_exp, t = logits_t.shape
    tb = _pick(t, (512, 256, 128))
    r = lax.broadcasted_iota(I32, (tb, tb), 0)
    c = lax.broadcasted_iota(I32, (tb, tb), 1)
    tri = (r < c).astype(BF16)
    row_spec = pl.BlockSpec((8, tb), lambda i: (0, i))
    return pl.pallas_call(
        _route_kernel,
        grid=(t // tb,),
        in_specs=[pl.BlockSpec((n_exp, tb), lambda i: (0, i)),
                  pl.BlockSpec((tb, tb), lambda i: (0, 0))],
        out_specs=[row_spec, row_spec, row_spec, pl.BlockSpec((n_exp, LANES), lambda i: (0, 0))],
        out_shape=[
            jax.ShapeDtypeStruct((8, t), I32),
            jax.ShapeDtypeStruct((8, t), F32),
            jax.ShapeDtypeStruct((8, t), I32),
            jax.ShapeDtypeStruct((n_exp, LANES), F32),
        ],
        scratch_shapes=[pltpu.VMEM((n_exp, 1), F32)],
        compiler_params=_cparams(("arbitrary",)),
        name="route",
    )(logits_t, tri)


def _dispatch_kernel(dest_ref, hp_ref, xs_in_ref, xs_ref, sem):
    del xs_in_ref
    tb = hp_ref.shape[0]

    def row_copy(t, d):
        return pltpu.make_async_copy(hp_ref.at[pl.ds(t, 1), :], xs_ref.at[pl.ds(d, 1), :], sem)

    def issue(t, carry):
        for k in range(TOP_K):
            row_copy(t, dest_ref[0, k, t]).start()
        return carry

    lax.fori_loop(0, tb, issue, 0)

    def drain(t, carry):
        for k in range(TOP_K):
            row_copy(0, 0).wait()
        return carry

    lax.fori_loop(0, tb, drain, 0)


def _dispatch(dest3, hp, n_rows):
    t, half = hp.shape
    tb = dest3.shape[2]
    xs0 = jnp.zeros((n_rows, half), U32)
    return pl.pallas_call(
        _dispatch_kernel,
        grid=(t // tb,),
        in_specs=[
            pl.BlockSpec((1, TOP_K, tb), lambda i: (i, 0, 0), memory_space=pltpu.SMEM),
            pl.BlockSpec((tb, half), lambda i: (i, 0)),
            pl.BlockSpec(memory_space=pl.ANY),
        ],
        out_specs=pl.BlockSpec(memory_space=pl.ANY),
        out_shape=jax.ShapeDtypeStruct((n_rows, half), U32),
        scratch_shapes=[pltpu.SemaphoreType.DMA(())],
        input_output_aliases={2: 0},
        compiler_params=_cparams(("arbitrary",)),
        name="dispatch",
    )(dest3, hp, xs0)


def _expert_kernel(te_ref, nu_ref, xs_ref, wg_ref, wu_ref, bg_ref, bu_ref, wd_ref, bd_ref,
                   out_ref, acc, x_scr):
    del te_ref
    i = pl.program_id(0)
    j = pl.program_id(1)

    @pl.when(i < nu_ref[0])
    def _():
        @pl.when(j == 0)
        def _():
            words = xs_ref[...]
            half = words.shape[1]
            x_scr[:, :half] = lax.bitcast_convert_type(words << 16, F32).astype(BF16)
            x_scr[:, half:] = lax.bitcast_convert_type(words & jnp.uint32(0xFFFF0000), F32).astype(BF16)
            acc[...] = jnp.broadcast_to(bd_ref[0], acc.shape)

        x = x_scr[...]
        gate = jnp.dot(x, wg_ref[0], preferred_element_type=F32) + bg_ref[0]
        up = jnp.dot(x, wu_ref[0], preferred_element_type=F32) + bu_ref[0]
        gate = jnp.minimum(gate, SWIGLU_LIMIT)
        up = jnp.clip(up, -SWIGLU_LIMIT, SWIGLU_LIMIT)
        act = ((up + 1.0) * gate * _sigmoid(SWIGLU_ALPHA * gate)).astype(BF16)
        acc[...] += jnp.dot(act, wd_ref[0], preferred_element_type=F32)

        @pl.when(j == pl.num_programs(1) - 1)
        def _():
            out_ref[...] = acc[...]

    @pl.when((i >= nu_ref[0]) & (j == pl.num_programs(1) - 1))
    def _():
        out_ref[...] = jnp.zeros_like(out_ref)


def _experts(tile_expert, n_used, xs, w_gate_up, b_gate_up3, w_down, b_down3, *, tm):
    n_rows, half = xs.shape
    d = 2 * half
    n_exp, _, two_f = w_gate_up.shape
    f = two_f // 2
    tf = _pick(f, (512, 256, 128))
    nf = f // tf
    n_tiles = n_rows // tm

    def tile(i, nu):
        return jnp.minimum(i, nu[0] - 1)

    grid_spec = pltpu.PrefetchScalarGridSpec(
        num_scalar_prefetch=2,
        grid=(n_tiles, nf),
        in_specs=[
            pl.BlockSpec((tm, half), lambda i, j, te, nu: (tile(i, nu), 0)),
            pl.BlockSpec((1, d, tf), lambda i, j, te, nu: (te[tile(i, nu)], 0, j)),
            pl.BlockSpec((1, d, tf), lambda i, j, te, nu: (te[tile(i, nu)], 0, nf + j)),
            pl.BlockSpec((1, 1, tf), lambda i, j, te, nu: (te[tile(i, nu)], 0, j)),
            pl.BlockSpec((1, 1, tf), lambda i, j, te, nu: (te[tile(i, nu)], 0, nf + j)),
            pl.BlockSpec((1, tf, d), lambda i, j, te, nu: (te[tile(i, nu)], j, 0)),
            pl.BlockSpec((1, 1, d), lambda i, j, te, nu: (te[tile(i, nu)], 0, 0)),
        ],
        out_specs=pl.BlockSpec((tm, d), lambda i, j, te, nu: (i, 0)),
        scratch_shapes=[pltpu.VMEM((tm, d), F32), pltpu.VMEM((tm, d), BF16)],
    )
    return pl.pallas_call(
        _expert_kernel,
        grid_spec=grid_spec,
        out_shape=jax.ShapeDtypeStruct((n_rows, d), F32),
        compiler_params=_cparams(("arbitrary", "arbitrary")),
        name="experts",
    )(tile_expert, n_used, xs, w_gate_up, w_gate_up, b_gate_up3, b_gate_up3, w_down, b_down3)


def _combine_kernel(dest_ref, x1_ref, gates_ref, gf_ref, ys_ref, y_ref, gbuf, sem):
    tb = x1_ref.shape[0]

    def row_copy(k, t, d):
        return pltpu.make_async_copy(ys_ref.at[pl.ds(d, 1), :], gbuf.at[k, pl.ds(t, 1), :], sem)

    def issue(t, carry):
        for k in range(TOP_K):
            row_copy(k, t, dest_ref[0, k, t]).start()
        return carry

    lax.fori_loop(0, tb, issue, 0)

    def drain(t, carry):
        for k in range(TOP_K):
            row_copy(0, 0, 0).wait()
        return carry

    lax.fori_loop(0, tb, drain, 0)

    gates = gates_ref[...]
    y = x1_ref[...]
    for k in range(TOP_K):
        y = y + gates[:, k:k + 1] * gbuf[k]
    y_ref[...] = _rms(y, gf_ref[...])


def _combine(dest3, x1, gates_tk, gf, ys):
    t, d = x1.shape
    tb = dest3.shape[2]
    return pl.pallas_call(
        _combine_kernel,
        grid=(t // tb,),
        in_specs=[
            pl.BlockSpec((1, TOP_K, tb), lambda i: (i, 0, 0), memory_space=pltpu.SMEM),
            pl.BlockSpec((tb, d), lambda i: (i, 0)),
            pl.BlockSpec((tb, TOP_K), lambda i: (i, 0)),
            pl.BlockSpec((1, d), lambda i: (0, 0)),
            pl.BlockSpec(memory_space=pl.ANY),
        ],
        out_specs=pl.BlockSpec((tb, d), lambda i: (i, 0)),
        out_shape=jax.ShapeDtypeStruct((t, d), F32),
        scratch_shapes=[pltpu.VMEM((TOP_K, tb, d), F32), pltpu.SemaphoreType.DMA(())],
        compiler_params=_cparams(("arbitrary",)),
        name="combine",
    )(dest3, x1, gates_tk, gf, ys)


def kernel(x_prompt, x_sample, state_gdn, cache_gdn_conv, state_hgrn, ln1_g, w_in, gdn_conv_w,
           gdn_A_log, gdn_dt_bias, gdn_norm_g, hgrn_lb_logits, hgrn_norm_g, w_out, ln2_g,
           w_router, b_router, w_gate_up, b_gate_up, w_down, b_down, ln_f_g):
    nb, seq, d = x_prompt.shape
    db, dseq, _ = x_sample.shape
    depth, _, heads, dk, dv = state_gdn.shape
    assert depth == 1 and dk == HEAD_DIM and dv == HEAD_DIM
    assert state_hgrn.shape[2:] == (heads, HEAD_DIM, HEAD_DIM)
    assert seq % CHUNK == 0 and dseq % CHUNK == 0 and 2 * heads <= LANES
    w = heads * HEAD_DIM
    assert d == 2 * w and w_in.shape[2] == 8 * w + 2 * heads
    n_exp = w_router.shape[2]
    t_p = nb * seq
    t_all = t_p + db * dseq
    n_seq = nb + db

    wi = w_in[0]
    w_main = jnp.concatenate([wi[:, :4 * w], wi[:, 4 * w + 2 * heads:]], axis=1).astype(BF16)
    w_small = jnp.pad(wi[:, 4 * w:4 * w + 2 * heads], ((0, 0), (0, LANES - 2 * heads))).astype(BF16)
    conv_w8 = jnp.pad(gdn_conv_w[0], ((0, 8 - CONV_W), (0, 0)))
    lane_pad = lambda v: jnp.pad(v, (heads, LANES - 2 * heads))
    ad = jnp.pad(jnp.stack([lane_pad(gdn_A_log[0]), lane_pad(gdn_dt_bias[0])]), ((0, 6), (0, 0)))
    lower_bounds = jnp.cumsum(jax.nn.softmax(hgrn_lb_logits.astype(F32), axis=0), axis=0)[0:1]
    zeros_state = jnp.zeros((nb, heads, HEAD_DIM, HEAD_DIM), F32)
    sg0 = jnp.concatenate([zeros_state, state_gdn[0]], axis=0)
    sh0 = jnp.concatenate([zeros_state, state_hgrn[0]], axis=0)
    cv0 = jnp.pad(jnp.concatenate([jnp.zeros((nb, CONV_W - 1, 3 * w), F32), cache_gdn_conv[0]], axis=0),
                  ((0, 0), (8 - (CONV_W - 1), 0), (0, 0)))

    x_all = jnp.concatenate([x_prompt.reshape(t_p, d), x_sample.reshape(db * dseq, d)], axis=0)

    pm, ps = _inproj(x_all, ln1_g, w_main, w_small)
    mixed, sg, sh, cv = _mixer(pm, ps, conv_w8, ad, gdn_norm_g, lower_bounds, hgrn_norm_g,
                               sg0, sh0, cv0, heads=heads, n_prompt_seq=nb,
                               n_prompt_chunks=seq // CHUNK, n_sample_chunks=dseq // CHUNK)
    x1, hp, logits_t = _outproj(x_all, mixed, w_out[0].astype(BF16), ln2_g,
                                w_router[0].T, b_router[0][:, None])

    e_idx, gates, ranks, counts = _route(logits_t)
    tm_e = 512 if t_all * TOP_K >= 512 * n_exp else 128
    cnt = counts[:, 0].astype(I32)
    padded = (cnt + tm_e - 1) // tm_e * tm_e
    seg_end = jnp.cumsum(padded)
    seg_start = seg_end - padded
    n_tiles = (t_all * TOP_K + n_exp * (tm_e - 1)) // tm_e
    dest = seg_start[e_idx[:TOP_K]] + ranks[:TOP_K]
    tb = _pick(t_all, (256, 128))
    dest3 = dest.reshape(TOP_K, t_all // tb, tb).transpose(1, 0, 2)
    tile_expert = jnp.minimum(
        jnp.searchsorted(seg_end, jnp.arange(n_tiles, dtype=I32) * tm_e, side="right"),
        n_exp - 1).astype(I32)
    n_used = (seg_end[-1:] // tm_e).astype(I32)

    xs = _dispatch(dest3, hp, n_tiles * tm_e)
    ys = _experts(tile_expert, n_used, xs, w_gate_up[0].astype(BF16), b_gate_up[0][:, None, :],
                  w_down[0].astype(BF16), b_down[0][:, None, :], tm=tm_e)
    y_all = _combine(dest3, x1, gates[:TOP_K].T, ln_f_g[None, :], ys)

    y_prompt = y_all[:t_p].reshape(nb, seq, d)
    y_sample = y_all[t_p:].reshape(db, dseq, d)
    cache = cv[:, 8 - (CONV_W - 1):, :]
    return (y_prompt, y_sample,
            sg[None, :nb], cache[None, :nb], sh[None, :nb],
            sg[None, nb:], cache[None, nb:], sh[None, nb:])
```

```python
import functools

import jax
import jax.numpy as jnp
from jax import lax
from jax.experimental import pallas as pl
from jax.experimental.pallas import tpu as pltpu

F32 = jnp.float32
BF16 = jnp.bfloat16
U32 = jnp.uint32
I32 = jnp.int32

CHUNK = 64
CONV_W = 4
TOP_K = 4
NORM_EPS = 1e-6
L2_EPS = 1e-6
SWIGLU_LIMIT = 7.0
SWIGLU_ALPHA = 1.702
HEAD_DIM = 128
LANES = 128
HGRN_SUB = 16
VMEM_LIMIT = 56 * 1024 * 1024


def _pick(n, cands):
    for c in cands:
        if n % c == 0:
            return c
    raise ValueError(f"no tile in {cands} divides {n}")


def _cparams(sem):
    return pltpu.CompilerParams(dimension_semantics=sem, vmem_limit_bytes=VMEM_LIMIT)


def _sigmoid(x):
    return 1.0 / (1.0 + jnp.exp(-x))


def _silu(x):
    return x * _sigmoid(x)


def _softplus(x):
    return jnp.maximum(x, 0.0) + jnp.log1p(jnp.exp(-jnp.abs(x)))


def _rms(x, g):
    return x * lax.rsqrt(jnp.mean(x * x, axis=-1, keepdims=True) + NORM_EPS) * g


def _dot(a, b):
    return jnp.dot(a.astype(BF16), b.astype(BF16), preferred_element_type=F32)


def _dot_nt(a, b):
    return lax.dot_general(a.astype(BF16), b.astype(BF16), (((1,), (1,)), ((), ())),
                           preferred_element_type=F32)


def _dot_tn(a, b):
    return lax.dot_general(a.astype(BF16), b.astype(BF16), (((0,), (0,)), ((), ())),
                           preferred_element_type=F32)


def _inproj_kernel(x_ref, g_ref, wm_ref, ws_ref, pm_ref, ps_ref, h_scr):
    @pl.when(pl.program_id(1) == 0)
    def _():
        hb = _rms(x_ref[...], g_ref[...]).astype(BF16)
        h_scr[...] = hb
        ps_ref[...] = jnp.dot(hb, ws_ref[...], preferred_element_type=F32)

    pm_ref[...] = jnp.dot(h_scr[...], wm_ref[...], preferred_element_type=F32)


def _inproj(x, g, w_main, w_small):
    t, d = x.shape
    nm = w_main.shape[1]
    tm = _pick(t, (1024, 512, 256, 128, 64))
    tn = _pick(nm, (1024, 512, 256, 128))
    return pl.pallas_call(
        _inproj_kernel,
        grid=(t // tm, nm // tn),
        in_specs=[
            pl.BlockSpec((tm, d), lambda i, j: (i, 0)),
            pl.BlockSpec((1, d), lambda i, j: (0, 0)),
            pl.BlockSpec((d, tn), lambda i, j: (0, j)),
            pl.BlockSpec((d, LANES), lambda i, j: (0, 0)),
        ],
        out_specs=[
            pl.BlockSpec((tm, tn), lambda i, j: (i, j)),
            pl.BlockSpec((tm, LANES), lambda i, j: (i, 0)),
        ],
        out_shape=[jax.ShapeDtypeStruct((t, nm), F32), jax.ShapeDtypeStruct((t, LANES), F32)],
        scratch_shapes=[pltpu.VMEM((tm, d), BF16)],
        compiler_params=_cparams(("arbitrary", "arbitrary")),
        name="inproj",
    )(x, g, w_main, w_small)


def _cumsum_rows(x, row_idx):
    s = 1
    while s < x.shape[0]:
        x = x + jnp.where(row_idx >= s, pltpu.roll(x, s, 0), 0.0)
        s *= 2
    return x


def _mixer_kernel(heads, n_prompt_chunks, n_prompt_total, n_sample_chunks,
                  pm_ref, ps_ref, cw_ref, ad_ref, gng_ref, lb_ref, hng_ref,
                  sg0_ref, sh0_ref, cv0_ref,
                  mix_ref, sg_ref, sh_ref, cv_ref, cbuf, qkv_scr):
    c = pl.program_id(0)
    w = heads * HEAD_DIM
    L = CHUNK
    is_start = jnp.where(c < n_prompt_total, (c % n_prompt_chunks) == 0,
                         ((c - n_prompt_total) % n_sample_chunks) == 0)

    @pl.when(is_start)
    def _():
        sg_ref[...] = sg0_ref[...]
        sh_ref[...] = sh0_ref[...]
        cbuf[0:8, :] = cv0_ref[0]

    cbuf[8:8 + L, :] = pm_ref[:, 0:3 * w]
    conv = cbuf[5:5 + L, :] * cw_ref[0:1, :]
    for j in range(1, CONV_W):
        conv = conv + cbuf[5 + j:5 + j + L, :] * cw_ref[j:j + 1, :]
    tail = cbuf[L:L + 8, :]
    cv_ref[0] = tail
    cbuf[0:8, :] = tail
    qkv_scr[...] = _silu(conv)

    row = lax.broadcasted_iota(I32, (L, L), 0)
    col = lax.broadcasted_iota(I32, (L, L), 1)
    incl = row >= col
    strict = row > col
    eye = (row == col).astype(F32)
    lvl_masks = []
    for lg2 in range(L.bit_length() - 1):
        same_pair = (row >> (lg2 + 1)) == (col >> (lg2 + 1))
        lower_left = (((row >> lg2) & 1) == 1) & (((col >> lg2) & 1) == 0)
        lvl_masks.append(same_pair & lower_left)
    row_l = lax.broadcasted_iota(I32, (L, HEAD_DIM), 0)
    sub_row = lax.broadcasted_iota(I32, (HGRN_SUB, L), 0)
    sub_col = lax.broadcasted_iota(I32, (HGRN_SUB, L), 1)
    hs = range(heads)
    lane = lambda base, h: slice(base + h * HEAD_DIM, base + (h + 1) * HEAD_DIM)

    def gated_norm(o, g_ref, z):
        return o * lax.rsqrt(jnp.mean(o * o, axis=-1, keepdims=True) + NORM_EPS) * g_ref[...] * _silu(z)

    def hgrn_front(h):
        qh = _silu(pm_ref[:, lane(4 * w, h)])
        lbh = lb_ref[:, lane(0, h)]
        f = lbh + (1.0 - lbh) * _sigmoid(pm_ref[:, lane(5 * w, h)])
        kh = 1.0 - f
        v = pm_ref[:, lane(6 * w, h)]
        bc = _cumsum_rows(jnp.log(f), row_l)
        a_parts = []
        for i in range(L // HGRN_SUB):
            r0 = i * HGRN_SUB
            r1 = r0 + HGRN_SUB
            e_i = bc[r0 - 1:r0, :] if i > 0 else jnp.zeros((1, HEAD_DIM), F32)
            qt = qh[r0:r1] * jnp.exp(bc[r0:r1] - e_i)
            kt = kh * jnp.exp(jnp.where(row_l < r1, e_i - bc, 0.0))
            a_parts.append(jnp.where(sub_col <= sub_row + r0, _dot_nt(qt, kt), 0.0))
        s = sh_ref[0, h]
        o_inter = _dot(qh * jnp.exp(bc), s)
        ft_col = jnp.exp(bc.T[:, L - 1:L])
        sh_ref[0, h] = ft_col * s + _dot_tn(kh * jnp.exp(bc[L - 1:L, :] - bc), v)
        return jnp.concatenate(a_parts, axis=0), v, o_inter

    def hgrn_back(h, front):
        attn_h, v, o_inter = front
        o = gated_norm(o_inter + _dot(attn_h, v), hng_ref, pm_ref[:, lane(7 * w, h)])
        mix_ref[:, lane(w, h)] = o.astype(mix_ref.dtype)

    fronts = {}

    def hgrn_slot(i):
        if i < heads:
            fronts[i] = hgrn_front(i)
        if 0 <= i - 1 < heads:
            hgrn_back(i - 1, fronts.pop(i - 1))

    ps = ps_ref[...]
    beta_all = _sigmoid(ps)
    g_all = -jnp.exp(ad_ref[0:1, :]) * _softplus(ps + ad_ref[1:2, :])
    gc_all = _cumsum_rows(g_all, row_l)
    gc_t = gc_all.T
    q, k, v, bcol, gcol, egc, glast, dec_incl = ([] for _ in range(8))
    for h in hs:
        q_c = qkv_scr[:, lane(0, h)]
        k_c = qkv_scr[:, lane(w, h)]
        v.append(qkv_scr[:, lane(2 * w, h)])
        q.append(q_c * lax.rsqrt(jnp.sum(q_c * q_c, axis=-1, keepdims=True) + L2_EPS) * (HEAD_DIM ** -0.5))
        k.append(k_c * lax.rsqrt(jnp.sum(k_c * k_c, axis=-1, keepdims=True) + L2_EPS))
        bcol.append(beta_all[:, h:h + 1])
        gcol.append(gc_all[:, heads + h:heads + h + 1])
        glast.append(gc_all[L - 1:L, heads + h:heads + h + 1])
        egc.append(jnp.exp(gcol[h]))
        grow = gc_t[heads + h:heads + h + 1, :]
        dec_incl.append(jnp.where(incl, jnp.exp(jnp.where(incl, gcol[h] - grow, 0.0)), 0.0))
    kk = [_dot_nt(k[h], k[h]) for h in hs]
    qk = [_dot_nt(q[h], k[h]) for h in hs]
    a_mat = [bcol[h] * kk[h] * jnp.where(strict, dec_incl[h], 0.0) for h in hs]
    attn = [qk[h] * dec_incl[h] for h in hs]
    s_old = [sg_ref[0, h] for h in hs]
    o_inter = [_dot(q[h] * egc[h], s_old[h]) for h in hs]
    x = [eye - jnp.where(lvl_masks[0], a_mat[h], 0.0) for h in hs]
    slot = 0
    for m in lvl_masks[1:]:
        y = [_dot(x[h], jnp.where(m, a_mat[h], 0.0)) for h in hs]
        hgrn_slot(slot)
        x = [x[h] - _dot(y[h], x[h]) for h in hs]
        hgrn_slot(slot + 1)
        slot += 2
    sol = [_dot(x[h], jnp.concatenate([bcol[h] * v[h], (bcol[h] * egc[h]) * k[h]], axis=-1)) for h in hs]
    while slot <= heads:
        hgrn_slot(slot)
        slot += 1
    delta = [sol[h][:, :HEAD_DIM] - _dot(sol[h][:, HEAD_DIM:], s_old[h]) for h in hs]
    o_intra = [_dot(attn[h], delta[h]) for h in hs]
    s_add = [_dot_tn(k[h] * jnp.exp(glast[h] - gcol[h]), delta[h]) for h in hs]
    for h in hs:
        sg_ref[0, h] = jnp.exp(glast[h]) * s_old[h] + s_add[h]
        o = gated_norm(o_inter[h] + o_intra[h], gng_ref, pm_ref[:, lane(3 * w, h)])
        mix_ref[:, lane(0, h)] = o.astype(mix_ref.dtype)


def _mixer(pm, ps, conv_w, ad, gng, lb, hng, sg0, sh0, cv0, *, heads, n_prompt_seq,
           n_prompt_chunks, n_sample_chunks):
    t = pm.shape[0]
    nm = pm.shape[1]
    w = heads * HEAD_DIM
    n_seq = sg0.shape[0]
    n_chunks = t // CHUNK
    n_prompt_total = n_prompt_seq * n_prompt_chunks

    def seq_of(c):
        return jnp.where(c < n_prompt_total, c // n_prompt_chunks,
                         n_prompt_seq + (c - n_prompt_total) // n_sample_chunks)

    const = lambda shape: pl.BlockSpec(shape, lambda c: (0,) * len(shape))
    st_spec = pl.BlockSpec((1, heads, HEAD_DIM, HEAD_DIM), lambda c: (seq_of(c), 0, 0, 0))
    cv_spec = pl.BlockSpec((1, 8, 3 * w), lambda c: (seq_of(c), 0, 0))
    kern = functools.partial(_mixer_kernel, heads, n_prompt_chunks, n_prompt_total, n_sample_chunks)
    return pl.pallas_call(
        kern,
        grid=(n_chunks,),
        in_specs=[
            pl.BlockSpec((CHUNK, nm), lambda c: (c, 0)),
            pl.BlockSpec((CHUNK, LANES), lambda c: (c, 0)),
            const((8, 3 * w)), const((8, LANES)), const((1, HEAD_DIM)), const((1, w)),
            const((1, HEAD_DIM)),
            st_spec, st_spec, cv_spec,
        ],
        out_specs=[
            pl.BlockSpec((CHUNK, 2 * w), lambda c: (c, 0)),
            st_spec, st_spec, cv_spec,
        ],
        out_shape=[
            jax.ShapeDtypeStruct((t, 2 * w), BF16),
            jax.ShapeDtypeStruct((n_seq, heads, HEAD_DIM, HEAD_DIM), F32),
            jax.ShapeDtypeStruct((n_seq, heads, HEAD_DIM, HEAD_DIM), F32),
            jax.ShapeDtypeStruct((n_seq, 8, 3 * w), F32),
        ],
        scratch_shapes=[pltpu.VMEM((CHUNK + 8, 3 * w), F32), pltpu.VMEM((CHUNK, 3 * w), F32)],
        compiler_params=_cparams(("arbitrary",)),
        name="mixer",
    )(pm, ps, conv_w, ad, gng, lb, hng, sg0, sh0, cv0)


def _outproj_kernel(x_ref, m_ref, wo_ref, g_ref, wr_ref, br_ref, x1_ref, hp_ref, lg_ref):
    x1 = x_ref[...] + jnp.dot(m_ref[...], wo_ref[...], preferred_element_type=F32)
    x1_ref[...] = x1
    h2 = _rms(x1, g_ref[...])
    half = h2.shape[1] // 2
    bits = lax.bitcast_convert_type(h2.astype(BF16).astype(F32), U32)
    hp_ref[...] = (bits[:, :half] >> 16) | (bits[:, half:] & jnp.uint32(0xFFFF0000))
    wr = wr_ref[...]
    wr_hi = wr.astype(BF16)
    wr_lo = (wr - wr_hi.astype(F32)).astype(BF16)
    h_hi = h2.astype(BF16)
    h_lo = (h2 - h_hi.astype(F32)).astype(BF16)
    nt = lambda a, b: lax.dot_general(a, b, (((1,), (1,)), ((), ())), preferred_element_type=F32)
    lg_ref[...] = nt(wr_hi, h_hi) + nt(wr_hi, h_lo) + nt(wr_lo, h_hi) + br_ref[...]


def _outproj(x, mixed, w_out, g2, w_router_t, b_router_col):
    t, d = x.shape
    e = w_router_t.shape[0]
    tm = _pick(t, (512, 256, 128))
    return pl.pallas_call(
        _outproj_kernel,
        grid=(t // tm,),
        in_specs=[
            pl.BlockSpec((tm, d), lambda i: (i, 0)),
            pl.BlockSpec((tm, d), lambda i: (i, 0)),
            pl.BlockSpec((d, d), lambda i: (0, 0)),
            pl.BlockSpec((1, d), lambda i: (0, 0)),
            pl.BlockSpec((e, d), lambda i: (0, 0)),
            pl.BlockSpec((e, 1), lambda i: (0, 0)),
        ],
        out_specs=[
            pl.BlockSpec((tm, d), lambda i: (i, 0)),
            pl.BlockSpec((tm, d // 2), lambda i: (i, 0)),
            pl.BlockSpec((e, tm), lambda i: (0, i)),
        ],
        out_shape=[
            jax.ShapeDtypeStruct((t, d), F32),
            jax.ShapeDtypeStruct((t, d // 2), U32),
            jax.ShapeDtypeStruct((e, t), F32),
        ],
        compiler_params=_cparams(("arbitrary",)),
        name="outproj",
    )(x, mixed, w_out, g2, w_router_t, b_router_col)


def _route_kernel(lg_ref, tri_ref, e_ref, gate_ref, rank_ref, cnt_ref, carry):
    @pl.when(pl.program_id(0) == 0)
    def _():
        carry[...] = jnp.zeros_like(carry)

    lg = lg_ref[...]
    n_exp, tb = lg.shape
    eio = lax.broadcasted_iota(I32, (n_exp, tb), 0).astype(F32)
    work = lg
    vals, onehots = [], []
    for k in range(TOP_K):
        m = jnp.max(work, axis=0, keepdims=True)
        idx = jnp.min(jnp.where(work == m, eio, float(n_exp)), axis=0, keepdims=True)
        oh = eio == idx
        e_ref[k:k + 1, :] = idx.astype(I32)
        vals.append(m)
        onehots.append(oh)
        work = jnp.where(oh, -jnp.inf, work)
    exps = [jnp.exp(v - vals[0]) for v in vals]
    denom = exps[0] + exps[1] + exps[2] + exps[3]
    for k in range(TOP_K):
        gate_ref[k:k + 1, :] = exps[k] / denom
    self32 = sum(oh.astype(F32) for oh in onehots)
    before = carry[...] + jnp.dot(self32.astype(BF16), tri_ref[...], preferred_element_type=F32)
    for k in range(TOP_K):
        rank_ref[k:k + 1, :] = jnp.sum(jnp.where(onehots[k], before, 0.0), axis=0,
                                       keepdims=True).astype(I32)
    pad = jnp.zeros((8 - TOP_K, tb), I32)
    e_ref[TOP_K:8, :] = pad
    rank_ref[TOP_K:8, :] = pad
    gate_ref[TOP_K:8, :] = pad.astype(F32)
    carry[...] = carry[...] + jnp.sum(self32, axis=1, keepdims=True)
    cnt_ref[...] = jnp.broadcast_to(carry[...], cnt_ref.shape)


def _route(logits_t):
    n_exp, t = logits_t.shape
    tb = _pick(t, (512, 256, 128))
    r = lax.broadcasted_iota(I32, (tb, tb), 0)
    c = lax.broadcasted_iota(I32, (tb, tb), 1)
    tri = (r < c).astype(BF16)
    row_spec = pl.BlockSpec((8, tb), lambda i: (0, i))
    return pl.pallas_call(
        _route_kernel,
        grid=(t // tb,),
        in_specs=[pl.BlockSpec((n_exp, tb), lambda i: (0, i)),
                  pl.BlockSpec((tb, tb), lambda i: (0, 0))],
        out_specs=[row_spec, row_spec, row_spec, pl.BlockSpec((n_exp, LANES), lambda i: (0, 0))],
        out_shape=[
            jax.ShapeDtypeStruct((8, t), I32),
            jax.ShapeDtypeStruct((8, t), F32),
            jax.ShapeDtypeStruct((8, t), I32),
            jax.ShapeDtypeStruct((n_exp, LANES), F32),
        ],
        scratch_shapes=[pltpu.VMEM((n_exp, 1), F32)],
        compiler_params=_cparams(("arbitrary",)),
        name="route",
    )(logits_t, tri)


def _dispatch_kernel(dest_ref, hp_ref, xs_in_ref, xs_ref, sem):
    del xs_in_ref
    tb = hp_ref.shape[0]

    def row_copy(t, d):
        return pltpu.make_async_copy(hp_ref.at[pl.ds(t, 1), :], xs_ref.at[pl.ds(d, 1), :], sem)

    def issue(t, carry):
        for k in range(TOP_K):
            row_copy(t, dest_ref[0, k, t]).start()
        return carry

    lax.fori_loop(0, tb, issue, 0)

    def drain(t, carry):
        for k in range(TOP_K):
            row_copy(0, 0).wait()
        return carry

    lax.fori_loop(0, tb, drain, 0)


def _dispatch(dest3, hp, n_rows):
    t, half = hp.shape
    tb = dest3.shape[2]
    xs0 = jnp.zeros((n_rows, half), U32)
    return pl.pallas_call(
        _dispatch_kernel,
        grid=(t // tb,),
        in_specs=[
            pl.BlockSpec((1, TOP_K, tb), lambda i: (i, 0, 0), memory_space=pltpu.SMEM),
            pl.BlockSpec((tb, half), lambda i: (i, 0)),
            pl.BlockSpec(memory_space=pl.ANY),
        ],
        out_specs=pl.BlockSpec(memory_space=pl.ANY),
        out_shape=jax.ShapeDtypeStruct((n_rows, half), U32),
        scratch_shapes=[pltpu.SemaphoreType.DMA(())],
        input_output_aliases={2: 0},
        compiler_params=_cparams(("arbitrary",)),
        name="dispatch",
    )(dest3, hp, xs0)


def _expert_kernel(te_ref, nu_ref, xs_ref, wg_ref, wu_ref, bg_ref, bu_ref, wd_ref, bd_ref,
                   out_ref, acc, x_scr):
    del te_ref
    i = pl.program_id(0)
    j = pl.program_id(1)

    @pl.when(i < nu_ref[0])
    def _():
        @pl.when(j == 0)
        def _():
            words = xs_ref[...]
            half = words.shape[1]
            x_scr[:, :half] = lax.bitcast_convert_type(words << 16, F32).astype(BF16)
            x_scr[:, half:] = lax.bitcast_convert_type(words & jnp.uint32(0xFFFF0000), F32).astype(BF16)
            acc[...] = jnp.broadcast_to(bd_ref[0], acc.shape)

        x = x_scr[...]
        gate = jnp.dot(x, wg_ref[0], preferred_element_type=F32) + bg_ref[0]
        up = jnp.dot(x, wu_ref[0], preferred_element_type=F32) + bu_ref[0]
        gate = jnp.minimum(gate, SWIGLU_LIMIT)
        up = jnp.clip(up, -SWIGLU_LIMIT, SWIGLU_LIMIT)
        act = ((up + 1.0) * gate * _sigmoid(SWIGLU_ALPHA * gate)).astype(BF16)
        acc[...] += jnp.dot(act, wd_ref[0], preferred_element_type=F32)

        @pl.when(j == pl.num_programs(1) - 1)
        def _():
            out_ref[...] = acc[...]

    @pl.when((i >= nu_ref[0]) & (j == pl.num_programs(1) - 1))
    def _():
        out_ref[...] = jnp.zeros_like(out_ref)


def _experts(tile_expert, n_used, xs, w_gate_up, b_gate_up3, w_down, b_down3, *, tm):
    n_rows, half = xs.shape
    d = 2 * half
    n_exp, _, two_f = w_gate_up.shape
    f = two_f // 2
    tf = _pick(f, (1024, 512, 256, 128))
    nf = f // tf
    n_tiles = n_rows // tm

    def tile(i, nu):
        return jnp.minimum(i, nu[0] - 1)

    grid_spec = pltpu.PrefetchScalarGridSpec(
        num_scalar_prefetch=2,
        grid=(n_tiles, nf),
        in_specs=[
            pl.BlockSpec((tm, half), lambda i, j, te, nu: (tile(i, nu), 0)),
            pl.BlockSpec((1, d, tf), lambda i, j, te, nu: (te[tile(i, nu)], 0, j)),
            pl.BlockSpec((1, d, tf), lambda i, j, te, nu: (te[tile(i, nu)], 0, nf + j)),
            pl.BlockSpec((1, 1, tf), lambda i, j, te, nu: (te[tile(i, nu)], 0, j)),
            pl.BlockSpec((1, 1, tf), lambda i, j, te, nu: (te[tile(i, nu)], 0, nf + j)),
            pl.BlockSpec((1, tf, d), lambda i, j, te, nu: (te[tile(i, nu)], j, 0)),
            pl.BlockSpec((1, 1, d), lambda i, j, te, nu: (te[tile(i, nu)], 0, 0)),
        ],
        out_specs=pl.BlockSpec((tm, d), lambda i, j, te, nu: (i, 0)),
        scratch_shapes=[pltpu.VMEM((tm, d), F32), pltpu.VMEM((tm, d), BF16)],
    )
    return pl.pallas_call(
        _expert_kernel,
        grid_spec=grid_spec,
        out_shape=jax.ShapeDtypeStruct((n_rows, d), F32),
        compiler_params=_cparams(("arbitrary", "arbitrary")),
        name="experts",
    )(tile_expert, n_used, xs, w_gate_up, w_gate_up, b_gate_up3, b_gate_up3, w_down, b_down3)


def _combine_kernel(dest_ref, x1_ref, gates_ref, gf_ref, ys_ref, y_ref, gbuf, sem):
    tb = x1_ref.shape[0]

    def row_copy(k, t, d):
        return pltpu.make_async_copy(ys_ref.at[pl.ds(d, 1), :], gbuf.at[k, pl.ds(t, 1), :], sem)

    def issue(t, carry):
        for k in range(TOP_K):
            row_copy(k, t, dest_ref[0, k, t]).start()
        return carry

    lax.fori_loop(0, tb, issue, 0)

    def drain(t, carry):
        for k in range(TOP_K):
            row_copy(0, 0, 0).wait()
        return carry

    lax.fori_loop(0, tb, drain, 0)

    gates = gates_ref[...]
    y = x1_ref[...]
    for k in range(TOP_K):
        y = y + gates[:, k:k + 1] * gbuf[k]
    y_ref[...] = _rms(y, gf_ref[...])


def _combine(dest3, x1, gates_tk, gf, ys):
    t, d = x1.shape
    tb = dest3.shape[2]
    return pl.pallas_call(
        _combine_kernel,
        grid=(t // tb,),
        in_specs=[
            pl.BlockSpec((1, TOP_K, tb), lambda i: (i, 0, 0), memory_space=pltpu.SMEM),
            pl.BlockSpec((tb, d), lambda i: (i, 0)),
            pl.BlockSpec((tb, TOP_K), lambda i: (i, 0)),
            pl.BlockSpec((1, d), lambda i: (0, 0)),
            pl.BlockSpec(memory_space=pl.ANY),
        ],
        out_specs=pl.BlockSpec((tb, d), lambda i: (i, 0)),
        out_shape=jax.ShapeDtypeStruct((t, d), F32),
        scratch_shapes=[pltpu.VMEM((TOP_K, tb, d), F32), pltpu.SemaphoreType.DMA(())],
        compiler_params=_cparams(("arbitrary",)),
        name="combine",
    )(dest3, x1, gates_tk, gf, ys)


def kernel(x_prompt, x_sample, state_gdn, cache_gdn_conv, state_hgrn, ln1_g, w_in, gdn_conv_w,
           gdn_A_log, gdn_dt_bias, gdn_norm_g, hgrn_lb_logits, hgrn_norm_g, w_out, ln2_g,
           w_router, b_router, w_gate_up, b_gate_up, w_down, b_down, ln_f_g):
    nb, seq, d = x_prompt.shape
    db, dseq, _ = x_sample.shape
    depth, _, heads, dk, dv = state_gdn.shape
    assert depth == 1 and dk == HEAD_DIM and dv == HEAD_DIM
    assert state_hgrn.shape[2:] == (heads, HEAD_DIM, HEAD_DIM)
    assert seq % CHUNK == 0 and dseq % CHUNK == 0 and 2 * heads <= LANES
    w = heads * HEAD_DIM
    assert d == 2 * w and w_in.shape[2] == 8 * w + 2 * heads
    n_exp = w_router.shape[2]
    t_p = nb * seq
    t_all = t_p + db * dseq
    n_seq = nb + db

    wi = w_in[0]
    w_main = jnp.concatenate([wi[:, :4 * w], wi[:, 4 * w + 2 * heads:]], axis=1).astype(BF16)
    w_small = jnp.pad(wi[:, 4 * w:4 * w + 2 * heads], ((0, 0), (0, LANES - 2 * heads))).astype(BF16)
    conv_w8 = jnp.pad(gdn_conv_w[0], ((0, 8 - CONV_W), (0, 0)))
    lane_pad = lambda v: jnp.pad(v, (heads, LANES - 2 * heads))
    ad = jnp.pad(jnp.stack([lane_pad(gdn_A_log[0]), lane_pad(gdn_dt_bias[0])]), ((0, 6), (0, 0)))
    lower_bounds = jnp.cumsum(jax.nn.softmax(hgrn_lb_logits.astype(F32), axis=0), axis=0)[0:1]
    zeros_state = jnp.zeros((nb, heads, HEAD_DIM, HEAD_DIM), F32)
    sg0 = jnp.concatenate([zeros_state, state_gdn[0]], axis=0)
    sh0 = jnp.concatenate([zeros_state, state_hgrn[0]], axis=0)
    cv0 = jnp.pad(jnp.concatenate([jnp.zeros((nb, CONV_W - 1, 3 * w), F32), cache_gdn_conv[0]], axis=0),
                  ((0, 0), (8 - (CONV_W - 1), 0), (0, 0)))

    x_all = jnp.concatenate([x_prompt.reshape(t_p, d), x_sample.reshape(db * dseq, d)], axis=0)

    pm, ps = _inproj(x_all, ln1_g, w_main, w_small)
    mixed, sg, sh, cv = _mixer(pm, ps, conv_w8, ad, gdn_norm_g, lower_bounds, hgrn_norm_g,
                               sg0, sh0, cv0, heads=heads, n_prompt_seq=nb,
                               n_prompt_chunks=seq // CHUNK, n_sample_chunks=dseq // CHUNK)
    x1, hp, logits_t = _outproj(x_all, mixed, w_out[0].astype(BF16), ln2_g,
                                w_router[0].T, b_router[0][:, None])

    e_idx, gates, ranks, counts = _route(logits_t)
    tm_e = 512 if t_all * TOP_K >= 512 * n_exp else 128
    cnt = counts[:, 0].astype(I32)
    padded = (cnt + tm_e - 1) // tm_e * tm_e
    seg_end = jnp.cumsum(padded)
    seg_start = seg_end - padded
    n_tiles = (t_all * TOP_K + n_exp * (tm_e - 1)) // tm_e
    onehot = e_idx[:TOP_K, :, None] == jnp.arange(n_exp, dtype=I32)
    dest = jnp.sum(jnp.where(onehot, seg_start, 0), axis=-1) + ranks[:TOP_K]
    tb = _pick(t_all, (256, 128))
    dest3 = dest.reshape(TOP_K, t_all // tb, tb).transpose(1, 0, 2)
    tile_start = jnp.arange(n_tiles, dtype=I32) * tm_e
    tile_expert = jnp.minimum(jnp.sum(seg_end[None, :] <= tile_start[:, None], axis=1),
                              n_exp - 1).astype(I32)
    n_used = (seg_end[-1:] // tm_e).astype(I32)

    xs = _dispatch(dest3, hp, n_tiles * tm_e)
    ys = _experts(tile_expert, n_used, xs, w_gate_up[0].astype(BF16), b_gate_up[0][:, None, :],
                  w_down[0].astype(BF16), b_down[0][:, None, :], tm=tm_e)
    y_all = _combine(dest3, x1, gates[:TOP_K].T, ln_f_g[None, :], ys)

    y_prompt = y_all[:t_p].reshape(nb, seq, d)
    y_sample = y_all[t_p:].reshape(db, dseq, d)
    cache = cv[:, 8 - (CONV_W - 1):, :]
    return (y_prompt, y_sample,
            sg[None, :nb], cache[None, :nb], sh[None, :nb],
            sg[None, nb:], cache[None, nb:], sh[None, nb:])
```

```python
import functools
import math

import jax
import jax.numpy as jnp
from jax import lax
from jax.experimental import pallas as pl
from jax.experimental.pallas import tpu as pltpu

F32 = jnp.float32
BF16 = jnp.bfloat16
U32 = jnp.uint32
I32 = jnp.int32

CHUNK = 64
CONV_W = 4
TOP_K = 4
NORM_EPS = 1e-6
L2_EPS = 1e-6
SWIGLU_LIMIT = 7.0
SWIGLU_ALPHA = 1.702
HEAD_DIM = 128
LANES = 128
HGRN_SUB = 16
VMEM_LIMIT = 56 * 1024 * 1024


def _pick(n, cands):
    for c in cands:
        if n % c == 0:
            return c
    raise ValueError(f"no tile in {cands} divides {n}")


def _cparams(sem):
    return pltpu.CompilerParams(dimension_semantics=sem, vmem_limit_bytes=VMEM_LIMIT)


def _sigmoid(x):
    return 1.0 / (1.0 + jnp.exp(-x))


def _silu(x):
    return x * _sigmoid(x)


def _softplus(x):
    return jnp.maximum(x, 0.0) + jnp.log1p(jnp.exp(-jnp.abs(x)))


def _rms(x, g):
    return x * lax.rsqrt(jnp.mean(x * x, axis=-1, keepdims=True) + NORM_EPS) * g


def _dot(a, b):
    return jnp.dot(a.astype(BF16), b.astype(BF16), preferred_element_type=F32)


def _dot_nt(a, b):
    return lax.dot_general(a.astype(BF16), b.astype(BF16), (((1,), (1,)), ((), ())),
                           preferred_element_type=F32)


def _dot_tn(a, b):
    return lax.dot_general(a.astype(BF16), b.astype(BF16), (((0,), (0,)), ((), ())),
                           preferred_element_type=F32)


def _inproj_kernel(n_prompt_blocks, xp_ref, xs_ref, g_ref, wm_ref, ws_ref, pm_ref, ps_ref, h_scr):
    i = pl.program_id(0)
    first = pl.program_id(1) == 0

    def norm_from(x_ref):
        hb = _rms(x_ref[...], g_ref[...]).astype(BF16)
        h_scr[...] = hb
        ps_ref[...] = jnp.dot(hb, ws_ref[...], preferred_element_type=F32)

    pl.when(first & (i < n_prompt_blocks))(lambda: norm_from(xp_ref))
    pl.when(first & (i >= n_prompt_blocks))(lambda: norm_from(xs_ref))
    pm_ref[...] = jnp.dot(h_scr[...], wm_ref[...], preferred_element_type=F32)


def _two_stream_specs(tm, d, n_prompt_blocks, nargs):
    pick = (lambda f: (lambda i, j: f(i))) if nargs == 2 else (lambda f: f)
    return [
        pl.BlockSpec((tm, d), pick(lambda i: (jnp.minimum(i, n_prompt_blocks - 1), 0))),
        pl.BlockSpec((tm, d), pick(lambda i: (jnp.maximum(i - n_prompt_blocks, 0), 0))),
    ]


def _inproj(xp, xs, g, w_main, w_small):
    d = xp.shape[1]
    t = xp.shape[0] + xs.shape[0]
    nm = w_main.shape[1]
    tm = _pick(math.gcd(xp.shape[0], xs.shape[0]), (512, 256, 128, 64))
    tn = _pick(nm, (1024, 512, 256, 128))
    n_prompt_blocks = xp.shape[0] // tm
    return pl.pallas_call(
        functools.partial(_inproj_kernel, n_prompt_blocks),
        grid=(t // tm, nm // tn),
        in_specs=_two_stream_specs(tm, d, n_prompt_blocks, 2) + [
            pl.BlockSpec((1, d), lambda i, j: (0, 0)),
            pl.BlockSpec((d, tn), lambda i, j: (0, j)),
            pl.BlockSpec((d, LANES), lambda i, j: (0, 0)),
        ],
        out_specs=[
            pl.BlockSpec((tm, tn), lambda i, j: (i, j)),
            pl.BlockSpec((tm, LANES), lambda i, j: (i, 0)),
        ],
        out_shape=[jax.ShapeDtypeStruct((t, nm), F32), jax.ShapeDtypeStruct((t, LANES), F32)],
        scratch_shapes=[pltpu.VMEM((tm, d), BF16)],
        compiler_params=_cparams(("arbitrary", "arbitrary")),
        name="inproj",
    )(xp, xs, g, w_main, w_small)


def _cumsum_rows(x, row_idx):
    s = 1
    while s < x.shape[0]:
        x = x + jnp.where(row_idx >= s, pltpu.roll(x, s, 0), 0.0)
        s *= 2
    return x


def _mixer_kernel(heads, n_prompt_chunks, n_prompt_total, n_sample_chunks,
                  pm_ref, ps_ref, cw_ref, ad_ref, gng_ref, lb_ref, hng_ref,
                  sg0_ref, sh0_ref, cv0_ref,
                  mix_ref, sg_ref, sh_ref, cv_ref, cbuf, qkv_scr):
    c = pl.program_id(0)
    w = heads * HEAD_DIM
    L = CHUNK
    is_start = jnp.where(c < n_prompt_total, (c % n_prompt_chunks) == 0,
                         ((c - n_prompt_total) % n_sample_chunks) == 0)

    @pl.when(is_start)
    def _():
        sg_ref[...] = sg0_ref[...]
        sh_ref[...] = sh0_ref[...]
        cbuf[0:8, :] = cv0_ref[0]

    cbuf[8:8 + L, :] = pm_ref[:, 0:3 * w]
    conv = cbuf[5:5 + L, :] * cw_ref[0:1, :]
    for j in range(1, CONV_W):
        conv = conv + cbuf[5 + j:5 + j + L, :] * cw_ref[j:j + 1, :]
    tail = cbuf[L:L + 8, :]
    cv_ref[0] = tail
    cbuf[0:8, :] = tail
    qkv_scr[...] = _silu(conv)

    row = lax.broadcasted_iota(I32, (L, L), 0)
    col = lax.broadcasted_iota(I32, (L, L), 1)
    incl = row >= col
    strict = row > col
    eye = (row == col).astype(F32)
    lvl_masks = []
    for lg2 in range(L.bit_length() - 1):
        same_pair = (row >> (lg2 + 1)) == (col >> (lg2 + 1))
        lower_left = (((row >> lg2) & 1) == 1) & (((col >> lg2) & 1) == 0)
        lvl_masks.append(same_pair & lower_left)
    row_l = lax.broadcasted_iota(I32, (L, HEAD_DIM), 0)
    sub_row = lax.broadcasted_iota(I32, (HGRN_SUB, L), 0)
    sub_col = lax.broadcasted_iota(I32, (HGRN_SUB, L), 1)
    hs = range(heads)
    lane = lambda base, h: slice(base + h * HEAD_DIM, base + (h + 1) * HEAD_DIM)

    def gated_norm(o, g_ref, z):
        return o * lax.rsqrt(jnp.mean(o * o, axis=-1, keepdims=True) + NORM_EPS) * g_ref[...] * _silu(z)

    def hgrn_front(h):
        qh = _silu(pm_ref[:, lane(4 * w, h)])
        lbh = lb_ref[:, lane(0, h)]
        f = lbh + (1.0 - lbh) * _sigmoid(pm_ref[:, lane(5 * w, h)])
        kh = 1.0 - f
        v = pm_ref[:, lane(6 * w, h)]
        bc = _cumsum_rows(jnp.log(f), row_l)
        a_parts = []
        for i in range(L // HGRN_SUB):
            r0 = i * HGRN_SUB
            r1 = r0 + HGRN_SUB
            e_i = bc[r0 - 1:r0, :] if i > 0 else jnp.zeros((1, HEAD_DIM), F32)
            qt = qh[r0:r1] * jnp.exp(bc[r0:r1] - e_i)
            kt = kh * jnp.exp(jnp.where(row_l < r1, e_i - bc, 0.0))
            a_parts.append(jnp.where(sub_col <= sub_row + r0, _dot_nt(qt, kt), 0.0))
        s = sh_ref[0, h]
        o_inter = _dot(qh * jnp.exp(bc), s)
        ft_col = jnp.exp(bc.T[:, L - 1:L])
        sh_ref[0, h] = ft_col * s + _dot_tn(kh * jnp.exp(bc[L - 1:L, :] - bc), v)
        return jnp.concatenate(a_parts, axis=0), v, o_inter

    def hgrn_back(h, front):
        attn_h, v, o_inter = front
        o = gated_norm(o_inter + _dot(attn_h, v), hng_ref, pm_ref[:, lane(7 * w, h)])
        mix_ref[:, lane(w, h)] = o.astype(mix_ref.dtype)

    fronts = {}

    def hgrn_slot(i):
        if i < heads:
            fronts[i] = hgrn_front(i)
        if 0 <= i - 1 < heads:
            hgrn_back(i - 1, fronts.pop(i - 1))

    ps = ps_ref[...]
    beta_all = _sigmoid(ps)
    g_all = -jnp.exp(ad_ref[0:1, :]) * _softplus(ps + ad_ref[1:2, :])
    gc_all = _cumsum_rows(g_all, row_l)
    gc_t = gc_all.T
    q, k, v, bcol, gcol, egc, glast, dec_incl = ([] for _ in range(8))
    for h in hs:
        q_c = qkv_scr[:, lane(0, h)]
        k_c = qkv_scr[:, lane(w, h)]
        v.append(qkv_scr[:, lane(2 * w, h)])
        q.append(q_c * lax.rsqrt(jnp.sum(q_c * q_c, axis=-1, keepdims=True) + L2_EPS) * (HEAD_DIM ** -0.5))
        k.append(k_c * lax.rsqrt(jnp.sum(k_c * k_c, axis=-1, keepdims=True) + L2_EPS))
        bcol.append(beta_all[:, h:h + 1])
        gcol.append(gc_all[:, heads + h:heads + h + 1])
        glast.append(gc_all[L - 1:L, heads + h:heads + h + 1])
        egc.append(jnp.exp(gcol[h]))
        grow = gc_t[heads + h:heads + h + 1, :]
        dec_incl.append(jnp.where(incl, jnp.exp(jnp.where(incl, gcol[h] - grow, 0.0)), 0.0))
    kk = [_dot_nt(k[h], k[h]) for h in hs]
    qk = [_dot_nt(q[h], k[h]) for h in hs]
    a_mat = [bcol[h] * kk[h] * jnp.where(strict, dec_incl[h], 0.0) for h in hs]
    attn = [qk[h] * dec_incl[h] for h in hs]
    s_old = [sg_ref[0, h] for h in hs]
    o_inter = [_dot(q[h] * egc[h], s_old[h]) for h in hs]
    x = [eye - jnp.where(lvl_masks[0], a_mat[h], 0.0) for h in hs]
    slot = 0
    for m in lvl_masks[1:]:
        y = [_dot(x[h], jnp.where(m, a_mat[h], 0.0)) for h in hs]
        hgrn_slot(slot)
        x = [x[h] - _dot(y[h], x[h]) for h in hs]
        hgrn_slot(slot + 1)
        slot += 2
    sol = [_dot(x[h], jnp.concatenate([bcol[h] * v[h], (bcol[h] * egc[h]) * k[h]], axis=-1)) for h in hs]
    while slot <= heads:
        hgrn_slot(slot)
        slot += 1
    delta = [sol[h][:, :HEAD_DIM] - _dot(sol[h][:, HEAD_DIM:], s_old[h]) for h in hs]
    o_intra = [_dot(attn[h], delta[h]) for h in hs]
    s_add = [_dot_tn(k[h] * jnp.exp(glast[h] - gcol[h]), delta[h]) for h in hs]
    for h in hs:
        sg_ref[0, h] = jnp.exp(glast[h]) * s_old[h] + s_add[h]
        o = gated_norm(o_inter[h] + o_intra[h], gng_ref, pm_ref[:, lane(3 * w, h)])
        mix_ref[:, lane(0, h)] = o.astype(mix_ref.dtype)


def _mixer(pm, ps, conv_w, ad, gng, lb, hng, sg0, sh0, cv0, *, heads, n_prompt_seq,
           n_prompt_chunks, n_sample_chunks):
    t = pm.shape[0]
    nm = pm.shape[1]
    w = heads * HEAD_DIM
    n_seq = sg0.shape[0]
    n_chunks = t // CHUNK
    n_prompt_total = n_prompt_seq * n_prompt_chunks

    def seq_of(c):
        return jnp.where(c < n_prompt_total, c // n_prompt_chunks,
                         n_prompt_seq + (c - n_prompt_total) // n_sample_chunks)

    const = lambda shape: pl.BlockSpec(shape, lambda c: (0,) * len(shape))
    st_spec = pl.BlockSpec((1, heads, HEAD_DIM, HEAD_DIM), lambda c: (seq_of(c), 0, 0, 0))
    cv_spec = pl.BlockSpec((1, 8, 3 * w), lambda c: (seq_of(c), 0, 0))
    kern = functools.partial(_mixer_kernel, heads, n_prompt_chunks, n_prompt_total, n_sample_chunks)
    return pl.pallas_call(
        kern,
        grid=(n_chunks,),
        in_specs=[
            pl.BlockSpec((CHUNK, nm), lambda c: (c, 0)),
            pl.BlockSpec((CHUNK, LANES), lambda c: (c, 0)),
            const((8, 3 * w)), const((8, LANES)), const((1, HEAD_DIM)), const((1, w)),
            const((1, HEAD_DIM)),
            st_spec, st_spec, cv_spec,
        ],
        out_specs=[
            pl.BlockSpec((CHUNK, 2 * w), lambda c: (c, 0)),
            st_spec, st_spec, cv_spec,
        ],
        out_shape=[
            jax.ShapeDtypeStruct((t, 2 * w), BF16),
            jax.ShapeDtypeStruct((n_seq, heads, HEAD_DIM, HEAD_DIM), F32),
            jax.ShapeDtypeStruct((n_seq, heads, HEAD_DIM, HEAD_DIM), F32),
            jax.ShapeDtypeStruct((n_seq, 8, 3 * w), F32),
        ],
        scratch_shapes=[pltpu.VMEM((CHUNK + 8, 3 * w), F32), pltpu.VMEM((CHUNK, 3 * w), F32)],
        compiler_params=_cparams(("arbitrary",)),
        name="mixer",
    )(pm, ps, conv_w, ad, gng, lb, hng, sg0, sh0, cv0)


def _outproj_kernel(n_prompt_blocks, xp_ref, xs_ref, m_ref, wo_ref, g_ref, wr_ref, br_ref,
                    x1_ref, hp_ref, lg_ref):
    def body(x_ref):
        x1 = x_ref[...] + jnp.dot(m_ref[...], wo_ref[...], preferred_element_type=F32)
        x1_ref[...] = x1
        h2 = _rms(x1, g_ref[...])
        half = h2.shape[1] // 2
        bits = lax.bitcast_convert_type(h2.astype(BF16).astype(F32), U32)
        hp_ref[...] = (bits[:, :half] >> 16) | (bits[:, half:] & jnp.uint32(0xFFFF0000))
        wr = wr_ref[...]
        wr_hi = wr.astype(BF16)
        wr_lo = (wr - wr_hi.astype(F32)).astype(BF16)
        h_hi = h2.astype(BF16)
        h_lo = (h2 - h_hi.astype(F32)).astype(BF16)
        nt = lambda a, b: lax.dot_general(a, b, (((1,), (1,)), ((), ())), preferred_element_type=F32)
        lg_ref[...] = nt(wr_hi, h_hi) + nt(wr_hi, h_lo) + nt(wr_lo, h_hi) + br_ref[...]

    i = pl.program_id(0)
    pl.when(i < n_prompt_blocks)(lambda: body(xp_ref))
    pl.when(i >= n_prompt_blocks)(lambda: body(xs_ref))


def _outproj(xp, xs, mixed, w_out, g2, w_router_t, b_router_col):
    d = xp.shape[1]
    t = xp.shape[0] + xs.shape[0]
    e = w_router_t.shape[0]
    tm = _pick(math.gcd(xp.shape[0], xs.shape[0]), (512, 256, 128))
    n_prompt_blocks = xp.shape[0] // tm
    return pl.pallas_call(
        functools.partial(_outproj_kernel, n_prompt_blocks),
        grid=(t // tm,),
        in_specs=_two_stream_specs(tm, d, n_prompt_blocks, 1) + [
            pl.BlockSpec((tm, d), lambda i: (i, 0)),
            pl.BlockSpec((d, d), lambda i: (0, 0), pipeline_mode=pl.Buffered(1)),
            pl.BlockSpec((1, d), lambda i: (0, 0)),
            pl.BlockSpec((e, d), lambda i: (0, 0)),
            pl.BlockSpec((e, 1), lambda i: (0, 0)),
        ],
        out_specs=[
            pl.BlockSpec((tm, d), lambda i: (i, 0)),
            pl.BlockSpec((tm, d // 2), lambda i: (i, 0)),
            pl.BlockSpec((e, tm), lambda i: (0, i)),
        ],
        out_shape=[
            jax.ShapeDtypeStruct((t, d), F32),
            jax.ShapeDtypeStruct((t, d // 2), U32),
            jax.ShapeDtypeStruct((e, t), F32),
        ],
        compiler_params=_cparams(("arbitrary",)),
        name="outproj",
    )(xp, xs, mixed, w_out, g2, w_router_t, b_router_col)


def _route_kernel(lg_ref, tri_ref, e_ref, gate_ref, rank_ref, cnt_ref, carry):
    @pl.when(pl.program_id(0) == 0)
    def _():
        carry[...] = jnp.zeros_like(carry)

    lg = lg_ref[...]
    n_exp, tb = lg.shape
    eio = lax.broadcasted_iota(I32, (n_exp, tb), 0).astype(F32)
    work = lg
    vals, onehots = [], []
    for k in range(TOP_K):
        m = jnp.max(work, axis=0, keepdims=True)
        idx = jnp.min(jnp.where(work == m, eio, float(n_exp)), axis=0, keepdims=True)
        oh = eio == idx
        e_ref[k:k + 1, :] = idx.astype(I32)
        vals.append(m)
        onehots.append(oh)
        work = jnp.where(oh, -jnp.inf, work)
    exps = [jnp.exp(v - vals[0]) for v in vals]
    denom = exps[0] + exps[1] + exps[2] + exps[3]
    for k in range(TOP_K):
        gate_ref[k:k + 1, :] = exps[k] / denom
    self32 = sum(oh.astype(F32) for oh in onehots)
    before = carry[...] + jnp.dot(self32.astype(BF16), tri_ref[...], preferred_element_type=F32)
    for k in range(TOP_K):
        rank_ref[k:k + 1, :] = jnp.sum(jnp.where(onehots[k], before, 0.0), axis=0,
                                       keepdims=True).astype(I32)
    pad = jnp.zeros((8 - TOP_K, tb), I32)
    e_ref[TOP_K:8, :] = pad
    rank_ref[TOP_K:8, :] = pad
    gate_ref[TOP_K:8, :] = pad.astype(F32)
    carry[...] = carry[...] + jnp.sum(self32, axis=1, keepdims=True)
    cnt_ref[...] = jnp.broadcast_to(carry[...], cnt_ref.shape)


def _route(logits_t):
    n_exp, t = logits_t.shape
    tb = _pick(t, (512, 256, 128))
    r = lax.broadcasted_iota(I32, (tb, tb), 0)
    c = lax.broadcasted_iota(I32, (tb, tb), 1)
    tri = (r < c).astype(BF16)
    row_spec = pl.BlockSpec((8, tb), lambda i: (0, i))
    return pl.pallas_call(
        _route_kernel,
        grid=(t // tb,),
        in_specs=[pl.BlockSpec((n_exp, tb), lambda i: (0, i)),
                  pl.BlockSpec((tb, tb), lambda i: (0, 0))],
        out_specs=[row_spec, row_spec, row_spec, pl.BlockSpec((n_exp, LANES), lambda i: (0, 0))],
        out_shape=[
            jax.ShapeDtypeStruct((8, t), I32),
            jax.ShapeDtypeStruct((8, t), F32),
            jax.ShapeDtypeStruct((8, t), I32),
            jax.ShapeDtypeStruct((n_exp, LANES), F32),
        ],
        scratch_shapes=[pltpu.VMEM((n_exp, 1), F32)],
        compiler_params=_cparams(("arbitrary",)),
        name="route",
    )(logits_t, tri)


def _dispatch_kernel(pad_start_ref, pad_len_ref, tail_ref, dest_ref, hp_ref, xs_ref, zbuf, sem, zsem):
    tb = hp_ref.shape[0]
    zrows = zbuf.shape[0]

    @pl.when(pl.program_id(0) == 0)
    def _():
        zbuf[...] = jnp.zeros_like(zbuf)

        def pad_copy(r):
            return pltpu.make_async_copy(zbuf.at[pl.ds(0, 1), :], xs_ref.at[pl.ds(r, 1), :], zsem)

        def tail_copy(b):
            r = pl.multiple_of(tail_ref[0] + b * zrows, zrows)
            return pltpu.make_async_copy(zbuf, xs_ref.at[pl.ds(r, zrows), :], zsem)

        def for_pads(fn):
            def per_expert(e, carry):
                lax.fori_loop(0, pad_len_ref[e], lambda r, c: (fn(pad_start_ref[e] + r), c)[1], 0)
                return carry
            lax.fori_loop(0, pad_start_ref.shape[0], per_expert, 0)

        def for_tail(fn):
            lax.fori_loop(0, tail_ref[1], lambda b, c: (fn(b), c)[1], 0)

        for_pads(lambda r: pad_copy(r).start())
        for_tail(lambda b: tail_copy(b).start())
        for_pads(lambda r: pad_copy(0).wait())
        for_tail(lambda b: tail_copy(0).wait())

    def row_copy(t, d):
        return pltpu.make_async_copy(hp_ref.at[pl.ds(t, 1), :], xs_ref.at[pl.ds(d, 1), :], sem)

    def issue(t, carry):
        for k in range(TOP_K):
            row_copy(t, dest_ref[0, k, t]).start(priority=k % 2)
        return carry

    lax.fori_loop(0, tb, issue, 0)

    def drain(t, carry):
        for k in range(TOP_K):
            row_copy(0, 0).wait()
        return carry

    lax.fori_loop(0, tb, drain, 0)


def _dispatch(pad_start, pad_len, tail, dest3, hp, n_rows, zrows):
    t, half = hp.shape
    tb = dest3.shape[2]
    grid_spec = pltpu.PrefetchScalarGridSpec(
        num_scalar_prefetch=3,
        grid=(t // tb,),
        in_specs=[
            pl.BlockSpec((1, TOP_K, tb), lambda i, *_: (i, 0, 0), memory_space=pltpu.SMEM),
            pl.BlockSpec((tb, half), lambda i, *_: (i, 0)),
        ],
        out_specs=pl.BlockSpec(memory_space=pl.ANY),
        scratch_shapes=[pltpu.VMEM((zrows, half), U32), pltpu.SemaphoreType.DMA(()),
                        pltpu.SemaphoreType.DMA(())],
    )
    return pl.pallas_call(
        _dispatch_kernel,
        grid_spec=grid_spec,
        out_shape=jax.ShapeDtypeStruct((n_rows, half), U32),
        compiler_params=_cparams(("arbitrary",)),
        name="dispatch",
    )(pad_start, pad_len, tail, dest3, hp)


def _expert_kernel(nf, te_ref, nu_ref, xs_ref, wg_ref, wu_ref, bg_ref, bu_ref, wd_ref, bd_ref,
                   out_ref, acc, x_scr):
    del te_ref
    i = pl.program_id(0)
    j = pl.program_id(1)

    @pl.when(i < nu_ref[0])
    def _():
        @pl.when(j == 0)
        def _():
            words = xs_ref[...]
            half = words.shape[1]
            x_scr[:, :half] = lax.bitcast_convert_type(words << 16, F32).astype(BF16)
            x_scr[:, half:] = lax.bitcast_convert_type(words & jnp.uint32(0xFFFF0000), F32).astype(BF16)

        x = x_scr[...]
        gate = jnp.dot(x, wg_ref[0], preferred_element_type=F32) + bg_ref[0]
        up = jnp.dot(x, wu_ref[0], preferred_element_type=F32) + bu_ref[0]
        gate = jnp.minimum(gate, SWIGLU_LIMIT)
        up = jnp.clip(up, -SWIGLU_LIMIT, SWIGLU_LIMIT)
        act = ((up + 1.0) * gate * _sigmoid(SWIGLU_ALPHA * gate)).astype(BF16)
        part = jnp.dot(act, wd_ref[0], preferred_element_type=F32)
        if nf == 1:
            out_ref[...] = part + bd_ref[0]
        else:
            @pl.when(j == 0)
            def _():
                acc[...] = part + bd_ref[0]

            @pl.when((j > 0) & (j < nf - 1))
            def _():
                acc[...] += part

            @pl.when(j == nf - 1)
            def _():
                out_ref[...] = acc[...] + part

    @pl.when((i >= nu_ref[0]) & (j == nf - 1))
    def _():
        out_ref[...] = jnp.zeros_like(out_ref)


def _experts(tile_expert, n_used, xs, w_gate_up, b_gate_up3, w_down, b_down3, *, tm):
    n_rows, half = xs.shape
    d = 2 * half
    n_exp, _, two_f = w_gate_up.shape
    f = two_f // 2
    tf = _pick(f, (1024, 512, 256, 128))
    nf = f // tf
    n_tiles = n_rows // tm

    def tile(i, nu):
        return jnp.minimum(i, nu[0] - 1)

    grid_spec = pltpu.PrefetchScalarGridSpec(
        num_scalar_prefetch=2,
        grid=(n_tiles, nf),
        in_specs=[
            pl.BlockSpec((tm, half), lambda i, j, te, nu: (tile(i, nu), 0)),
            pl.BlockSpec((1, d, tf), lambda i, j, te, nu: (te[tile(i, nu)], 0, j)),
            pl.BlockSpec((1, d, tf), lambda i, j, te, nu: (te[tile(i, nu)], 0, nf + j)),
            pl.BlockSpec((1, 1, tf), lambda i, j, te, nu: (te[tile(i, nu)], 0, j)),
            pl.BlockSpec((1, 1, tf), lambda i, j, te, nu: (te[tile(i, nu)], 0, nf + j)),
            pl.BlockSpec((1, tf, d), lambda i, j, te, nu: (te[tile(i, nu)], j, 0)),
            pl.BlockSpec((1, 1, d), lambda i, j, te, nu: (te[tile(i, nu)], 0, 0)),
        ],
        out_specs=pl.BlockSpec((tm, d), lambda i, j, te, nu: (i, 0)),
        scratch_shapes=[pltpu.VMEM((tm, d), F32), pltpu.VMEM((tm, d), BF16)],
    )
    return pl.pallas_call(
        functools.partial(_expert_kernel, nf),
        grid_spec=grid_spec,
        out_shape=jax.ShapeDtypeStruct((n_rows, d), F32),
        compiler_params=_cparams(("arbitrary", "arbitrary")),
        name="experts",
    )(tile_expert, n_used, xs, w_gate_up, w_gate_up, b_gate_up3, b_gate_up3, w_down, b_down3)


def _combine_kernel(n, dest_ref, dest_next_ref, x1_ref, gates_ref, gf_ref, ys_ref, y_ref, gbuf, sems):
    i = pl.program_id(0)
    tb = x1_ref.shape[0]

    def row_copy(slot, k, t, d):
        return pltpu.make_async_copy(ys_ref.at[pl.ds(d, 1), :], gbuf.at[slot, k, pl.ds(t, 1), :],
                                     sems.at[slot])

    def gather(d_ref, slot):
        def issue(t, carry):
            for k in range(TOP_K):
                row_copy(slot, k, t, d_ref[0, k, t]).start(priority=k % 2)
            return carry
        lax.fori_loop(0, tb, issue, 0)

    pl.when(i == 0)(lambda: gather(dest_ref, 0))
    pl.when(i + 1 < n)(lambda: gather(dest_next_ref, (i + 1) % 2))

    slot = i % 2

    def drain(t, carry):
        for k in range(TOP_K):
            row_copy(slot, 0, 0, 0).wait()
        return carry

    lax.fori_loop(0, tb, drain, 0)

    gates = gates_ref[...]
    y = x1_ref[...]
    for k in range(TOP_K):
        y = y + gates[:, k:k + 1] * gbuf[slot, k]
    y_ref[...] = _rms(y, gf_ref[...])


def _combine(dest3, x1, gates_tk, gf, ys, blk0, n_blk):
    d = x1.shape[1]
    tb = dest3.shape[2]
    return pl.pallas_call(
        functools.partial(_combine_kernel, n_blk),
        grid=(n_blk,),
        in_specs=[
            pl.BlockSpec((1, TOP_K, tb), lambda i: (blk0 + i, 0, 0), memory_space=pltpu.SMEM),
            pl.BlockSpec((1, TOP_K, tb), lambda i: (blk0 + jnp.minimum(i + 1, n_blk - 1), 0, 0),
                         memory_space=pltpu.SMEM),
            pl.BlockSpec((tb, d), lambda i: (blk0 + i, 0)),
            pl.BlockSpec((tb, TOP_K), lambda i: (blk0 + i, 0)),
            pl.BlockSpec((1, d), lambda i: (0, 0)),
            pl.BlockSpec(memory_space=pl.ANY),
        ],
        out_specs=pl.BlockSpec((tb, d), lambda i: (i, 0)),
        out_shape=jax.ShapeDtypeStruct((n_blk * tb, d), F32),
        scratch_shapes=[pltpu.VMEM((2, TOP_K, tb, d), F32), pltpu.SemaphoreType.DMA((2,))],
        compiler_params=_cparams(("arbitrary",)),
        name="combine",
    )(dest3, dest3, x1, gates_tk, gf, ys)


def kernel(x_prompt, x_sample, state_gdn, cache_gdn_conv, state_hgrn, ln1_g, w_in, gdn_conv_w,
           gdn_A_log, gdn_dt_bias, gdn_norm_g, hgrn_lb_logits, hgrn_norm_g, w_out, ln2_g,
           w_router, b_router, w_gate_up, b_gate_up, w_down, b_down, ln_f_g):
    nb, seq, d = x_prompt.shape
    db, dseq, _ = x_sample.shape
    depth, _, heads, dk, dv = state_gdn.shape
    assert depth == 1 and dk == HEAD_DIM and dv == HEAD_DIM
    assert state_hgrn.shape[2:] == (heads, HEAD_DIM, HEAD_DIM)
    assert seq % CHUNK == 0 and dseq % CHUNK == 0 and 2 * heads <= LANES
    w = heads * HEAD_DIM
    assert d == 2 * w and w_in.shape[2] == 8 * w + 2 * heads
    n_exp = w_router.shape[2]
    t_p = nb * seq
    t_all = t_p + db * dseq
    n_seq = nb + db

    wi = w_in[0]
    w_main = jnp.concatenate([wi[:, :4 * w], wi[:, 4 * w + 2 * heads:]], axis=1).astype(BF16)
    w_small = jnp.pad(wi[:, 4 * w:4 * w + 2 * heads], ((0, 0), (0, LANES - 2 * heads))).astype(BF16)
    conv_w8 = jnp.pad(gdn_conv_w[0], ((0, 8 - CONV_W), (0, 0)))
    lane_pad = lambda v: jnp.pad(v, (heads, LANES - 2 * heads))
    ad = jnp.pad(jnp.stack([lane_pad(gdn_A_log[0]), lane_pad(gdn_dt_bias[0])]), ((0, 6), (0, 0)))
    lower_bounds = jnp.cumsum(jax.nn.softmax(hgrn_lb_logits.astype(F32), axis=0), axis=0)[0:1]
    zeros_state = jnp.zeros((nb, heads, HEAD_DIM, HEAD_DIM), F32)
    sg0 = jnp.concatenate([zeros_state, state_gdn[0]], axis=0)
    sh0 = jnp.concatenate([zeros_state, state_hgrn[0]], axis=0)
    cv0 = jnp.pad(jnp.concatenate([jnp.zeros((nb, CONV_W - 1, 3 * w), F32), cache_gdn_conv[0]], axis=0),
                  ((0, 0), (8 - (CONV_W - 1), 0), (0, 0)))

    xp = x_prompt.reshape(t_p, d)
    xs_tok = x_sample.reshape(db * dseq, d)

    pm, ps = _inproj(xp, xs_tok, ln1_g, w_main, w_small)
    mixed, sg, sh, cv = _mixer(pm, ps, conv_w8, ad, gdn_norm_g, lower_bounds, hgrn_norm_g,
                               sg0, sh0, cv0, heads=heads, n_prompt_seq=nb,
                               n_prompt_chunks=seq // CHUNK, n_sample_chunks=dseq // CHUNK)
    x1, hp, logits_t = _outproj(xp, xs_tok, mixed, w_out[0].astype(BF16), ln2_g,
                                w_router[0].T, b_router[0][:, None])

    e_idx, gates, ranks, counts = _route(logits_t)
    tm_e = 512 if t_all * TOP_K >= 512 * n_exp else 128
    cnt = counts[:, 0].astype(I32)
    padded = (cnt + tm_e - 1) // tm_e * tm_e
    seg_end = jnp.cumsum(padded)
    seg_start = seg_end - padded
    n_tiles = (t_all * TOP_K + n_exp * (tm_e - 1)) // tm_e
    onehot = e_idx[:TOP_K, :, None] == jnp.arange(n_exp, dtype=I32)
    dest = jnp.sum(jnp.where(onehot, seg_start, 0), axis=-1) + ranks[:TOP_K]
    tb = _pick(math.gcd(t_p, t_all - t_p), (256, 128))
    dest3 = dest.reshape(TOP_K, t_all // tb, tb).transpose(1, 0, 2)
    tile_start = jnp.arange(n_tiles, dtype=I32) * tm_e
    tile_expert = jnp.minimum(jnp.sum(seg_end[None, :] <= tile_start[:, None], axis=1),
                              n_exp - 1).astype(I32)
    n_used = (seg_end[-1:] // tm_e).astype(I32)

    zrows = 64
    tail = jnp.stack([seg_end[-1], (n_tiles * tm_e - seg_end[-1]) // zrows]).astype(I32)
    xs = _dispatch(seg_start + cnt, padded - cnt, tail, dest3, hp, n_tiles * tm_e, zrows)
    ys = _experts(tile_expert, n_used, xs, w_gate_up[0].astype(BF16), b_gate_up[0][:, None, :],
                  w_down[0].astype(BF16), b_down[0][:, None, :], tm=tm_e)
    gates_tk = gates[:TOP_K].T
    gf = ln_f_g[None, :]
    y_prompt = _combine(dest3, x1, gates_tk, gf, ys, 0, t_p // tb).reshape(nb, seq, d)
    y_sample = _combine(dest3, x1, gates_tk, gf, ys, t_p // tb, (t_all - t_p) // tb).reshape(db, dseq, d)
    cache = cv[:, 8 - (CONV_W - 1):, :]
    return (y_prompt, y_sample,
            sg[None, :nb], cache[None, :nb], sh[None, :nb],
            sg[None, nb:], cache[None, nb:], sh[None, nb:])
```

```python
import functools
import math

import jax
import jax.numpy as jnp
from jax import lax
from jax.experimental import pallas as pl
from jax.experimental.pallas import tpu as pltpu

F32 = jnp.float32
BF16 = jnp.bfloat16
U32 = jnp.uint32
I32 = jnp.int32

CHUNK = 64
CONV_W = 4
TOP_K = 4
NORM_EPS = 1e-6
L2_EPS = 1e-6
SWIGLU_LIMIT = 7.0
SWIGLU_ALPHA = 1.702
HEAD_DIM = 128
LANES = 128
HGRN_SUB = 16
VMEM_LIMIT = 56 * 1024 * 1024


def _pick(n, cands):
    for c in cands:
        if n % c == 0:
            return c
    raise ValueError(f"no tile in {cands} divides {n}")


def _cparams(sem):
    return pltpu.CompilerParams(dimension_semantics=sem, vmem_limit_bytes=VMEM_LIMIT)


def _sigmoid(x):
    return 1.0 / (1.0 + jnp.exp(-x))


def _silu(x):
    return x * _sigmoid(x)


def _softplus(x):
    return jnp.maximum(x, 0.0) + jnp.log1p(jnp.exp(-jnp.abs(x)))


def _rms(x, g):
    return x * lax.rsqrt(jnp.mean(x * x, axis=-1, keepdims=True) + NORM_EPS) * g


def _dot(a, b):
    return jnp.dot(a.astype(BF16), b.astype(BF16), preferred_element_type=F32)


def _dot_nt(a, b):
    return lax.dot_general(a.astype(BF16), b.astype(BF16), (((1,), (1,)), ((), ())),
                           preferred_element_type=F32)


def _dot_tn(a, b):
    return lax.dot_general(a.astype(BF16), b.astype(BF16), (((0,), (0,)), ((), ())),
                           preferred_element_type=F32)


GROUP = 4
PROJ_PIECES = 4


def _two_stream_specs(tm, d, n_prompt_blocks, block_of):
    return [
        pl.BlockSpec((tm, d), lambda s: (jnp.minimum(block_of(s), n_prompt_blocks - 1), 0)),
        pl.BlockSpec((tm, d), lambda s: (jnp.maximum(block_of(s) - n_prompt_blocks, 0), 0)),
    ]


def _cumsum_rows(x, row_idx):
    s = 1
    while s < x.shape[0]:
        x = x + jnp.where(row_idx >= s, pltpu.roll(x, s, 0), 0.0)
        s *= 2
    return x


def _front_kernel(heads, n_prompt_chunks, n_prompt_total, n_sample_chunks, n_prompt_groups,
                  xp_ref, xs_ref, g1_ref, wm_ref, ws_ref,
                  cw_ref, ad_ref, gng_ref, lb_ref, hng_ref, sg0_ref, sh0_ref, cv0_ref,
                  mix_ref, sg_ref, sh_ref, cv_ref,
                  h_scr, pm_a, ps_a, pm_b, ps_b, cbuf, qkv_scr):
    s = pl.program_id(0)
    g = s // GROUP

    @pl.when(s == 0)
    def _():
        for buf in (pm_a, ps_a, pm_b, ps_b):
            buf[...] = jnp.zeros_like(buf)

    args = (heads, n_prompt_chunks, n_prompt_total, n_sample_chunks, n_prompt_groups,
            xp_ref, xs_ref, g1_ref, wm_ref, ws_ref,
            cw_ref, ad_ref, gng_ref, lb_ref, hng_ref, sg0_ref, sh0_ref, cv0_ref,
            mix_ref, sg_ref, sh_ref, cv_ref, h_scr, cbuf, qkv_scr)
    pl.when(g % 2 == 0)(lambda: _front_step(*args, pm_a, ps_a, pm_b, ps_b))
    pl.when(g % 2 == 1)(lambda: _front_step(*args, pm_b, ps_b, pm_a, ps_a))


def _front_step(heads, n_prompt_chunks, n_prompt_total, n_sample_chunks, n_prompt_groups,
                xp_ref, xs_ref, g1_ref, wm_ref, ws_ref,
                cw_ref, ad_ref, gng_ref, lb_ref, hng_ref, sg0_ref, sh0_ref, cv0_ref,
                mix_ref, sg_ref, sh_ref, cv_ref, h_scr, cbuf, qkv_scr,
                pm_w, ps_w, pm_r, ps_r):
    s = pl.program_id(0)
    w = heads * HEAD_DIM
    L = CHUNK
    tw = wm_ref.shape[1]
    pw = tw // PROJ_PIECES
    g = s // GROUP
    j = s % GROUP
    c = jnp.maximum(s - GROUP, 0)
    r0 = pl.multiple_of((c % GROUP) * L, L)
    is_start = jnp.where(c < n_prompt_total, (c % n_prompt_chunks) == 0,
                         ((c - n_prompt_total) % n_sample_chunks) == 0)

    def norm_from(x_ref):
        hb = _rms(x_ref[...], g1_ref[...]).astype(BF16)
        h_scr[...] = hb
        ps_w[...] = jnp.dot(hb, ws_ref[...], preferred_element_type=F32)

    pl.when((j == 0) & (g < n_prompt_groups))(lambda: norm_from(xp_ref))
    pl.when((j == 0) & (g >= n_prompt_groups))(lambda: norm_from(xs_ref))

    def proj_piece(p):
        cols = slice(p * pw, (p + 1) * pw)
        pm_w[j, :, cols] = jnp.dot(h_scr[...], wm_ref[:, cols], preferred_element_type=F32)

    def pm(col0, width=HEAD_DIM):
        tile, off = divmod(col0, tw)
        assert off + width <= tw
        return pm_r[tile, pl.ds(r0, L), off:off + width]

    @pl.when(is_start)
    def _():
        sg_ref[...] = sg0_ref[...]
        sh_ref[...] = sh0_ref[...]
        cbuf[0:8, :] = cv0_ref[0]

    proj_piece(0)
    proj_piece(1)
    proj_piece(2)

    for c0 in range(0, 3 * w, tw):
        c1 = min(c0 + tw, 3 * w)
        cbuf[8:8 + L, c0:c1] = pm(c0, c1 - c0)
    conv = cbuf[5:5 + L, :] * cw_ref[0:1, :]
    for tap in range(1, CONV_W):
        conv = conv + cbuf[5 + tap:5 + tap + L, :] * cw_ref[tap:tap + 1, :]
    tail = cbuf[L:L + 8, :]
    cv_ref[0] = tail
    cbuf[0:8, :] = tail
    qkv_scr[...] = _silu(conv)

    row = lax.broadcasted_iota(I32, (L, L), 0)
    col = lax.broadcasted_iota(I32, (L, L), 1)
    incl = row >= col
    strict = row > col
    eye = (row == col).astype(F32)
    lvl_masks = []
    for lg2 in range(L.bit_length() - 1):
        same_pair = (row >> (lg2 + 1)) == (col >> (lg2 + 1))
        lower_left = (((row >> lg2) & 1) == 1) & (((col >> lg2) & 1) == 0)
        lvl_masks.append(same_pair & lower_left)
    row_l = lax.broadcasted_iota(I32, (L, HEAD_DIM), 0)
    sub_row = lax.broadcasted_iota(I32, (HGRN_SUB, L), 0)
    sub_col = lax.broadcasted_iota(I32, (HGRN_SUB, L), 1)
    hs = range(heads)
    lane = lambda base, h: slice(base + h * HEAD_DIM, base + (h + 1) * HEAD_DIM)

    def gated_norm(o, g_ref, z):
        return o * lax.rsqrt(jnp.mean(o * o, axis=-1, keepdims=True) + NORM_EPS) * g_ref[...] * _silu(z)

    def hgrn_front(h):
        qh = _silu(pm(4 * w + h * HEAD_DIM))
        lbh = lb_ref[:, lane(0, h)]
        f = lbh + (1.0 - lbh) * _sigmoid(pm(5 * w + h * HEAD_DIM))
        kh = 1.0 - f
        v = pm(6 * w + h * HEAD_DIM)
        bc = _cumsum_rows(jnp.log(f), row_l)
        a_parts = []
        for i in range(L // HGRN_SUB):
            r0 = i * HGRN_SUB
            r1 = r0 + HGRN_SUB
            e_i = bc[r0 - 1:r0, :] if i > 0 else jnp.zeros((1, HEAD_DIM), F32)
            qt = qh[r0:r1] * jnp.exp(bc[r0:r1] - e_i)
            kt = kh * jnp.exp(jnp.where(row_l < r1, e_i - bc, 0.0))
            a_parts.append(jnp.where(sub_col <= sub_row + r0, _dot_nt(qt, kt), 0.0))
        s = sh_ref[0, h]
        o_inter = _dot(qh * jnp.exp(bc), s)
        ft_col = jnp.exp(bc.T[:, L - 1:L])
        sh_ref[0, h] = ft_col * s + _dot_tn(kh * jnp.exp(bc[L - 1:L, :] - bc), v)
        return jnp.concatenate(a_parts, axis=0), v, o_inter

    def hgrn_back(h, front):
        attn_h, v, o_inter = front
        o = gated_norm(o_inter + _dot(attn_h, v), hng_ref, pm(7 * w + h * HEAD_DIM))
        mix_ref[:, lane(w, h)] = o.astype(mix_ref.dtype)

    fronts = {}

    def hgrn_slot(i):
        if i < heads:
            fronts[i] = hgrn_front(i)
        if 0 <= i - 1 < heads:
            hgrn_back(i - 1, fronts.pop(i - 1))

    ps = ps_r[pl.ds(r0, L), :]
    beta_all = _sigmoid(ps)
    g_all = -jnp.exp(ad_ref[0:1, :]) * _softplus(ps + ad_ref[1:2, :])
    gc_all = _cumsum_rows(g_all, row_l)
    gc_t = gc_all.T
    q, k, v, bcol, gcol, egc, glast, dec_incl = ([] for _ in range(8))
    for h in hs:
        q_c = qkv_scr[:, lane(0, h)]
        k_c = qkv_scr[:, lane(w, h)]
        v.append(qkv_scr[:, lane(2 * w, h)])
        q.append(q_c * lax.rsqrt(jnp.sum(q_c * q_c, axis=-1, keepdims=True) + L2_EPS) * (HEAD_DIM ** -0.5))
        k.append(k_c * lax.rsqrt(jnp.sum(k_c * k_c, axis=-1, keepdims=True) + L2_EPS))
        bcol.append(beta_all[:, h:h + 1])
        gcol.append(gc_all[:, heads + h:heads + h + 1])
        glast.append(gc_all[L - 1:L, heads + h:heads + h + 1])
        egc.append(jnp.exp(gcol[h]))
        grow = gc_t[heads + h:heads + h + 1, :]
        dec_incl.append(jnp.where(incl, jnp.exp(jnp.where(incl, gcol[h] - grow, 0.0)), 0.0))
    kk = [_dot_nt(k[h], k[h]) for h in hs]
    qk = [_dot_nt(q[h], k[h]) for h in hs]
    a_mat = [bcol[h] * kk[h] * jnp.where(strict, dec_incl[h], 0.0) for h in hs]
    attn = [qk[h] * dec_incl[h] for h in hs]
    s_old = [sg_ref[0, h] for h in hs]
    o_inter = [_dot(q[h] * egc[h], s_old[h]) for h in hs]
    x = [eye - jnp.where(lvl_masks[0], a_mat[h], 0.0) for h in hs]
    piece_after_level = {1: 3}
    assert PROJ_PIECES == 4 and len(lvl_masks) > 4
    slot = 0
    for lvl, m in enumerate(lvl_masks[1:]):
        y = [_dot(x[h], jnp.where(m, a_mat[h], 0.0)) for h in hs]
        hgrn_slot(slot)
        x = [x[h] - _dot(y[h], x[h]) for h in hs]
        hgrn_slot(slot + 1)
        slot += 2
        if lvl in piece_after_level:
            proj_piece(piece_after_level[lvl])
    sol = [_dot(x[h], jnp.concatenate([bcol[h] * v[h], (bcol[h] * egc[h]) * k[h]], axis=-1)) for h in hs]
    while slot <= heads:
        hgrn_slot(slot)
        slot += 1
    delta = [sol[h][:, :HEAD_DIM] - _dot(sol[h][:, HEAD_DIM:], s_old[h]) for h in hs]
    o_intra = [_dot(attn[h], delta[h]) for h in hs]
    s_add = [_dot_tn(k[h] * jnp.exp(glast[h] - gcol[h]), delta[h]) for h in hs]
    for h in hs:
        sg_ref[0, h] = jnp.exp(glast[h]) * s_old[h] + s_add[h]
        o = gated_norm(o_inter[h] + o_intra[h], gng_ref, pm(3 * w + h * HEAD_DIM))
        mix_ref[:, lane(0, h)] = o.astype(mix_ref.dtype)


def _front(xp, xs, g1, w_main, w_small, conv_w, ad, gng, lb, hng, sg0, sh0, cv0, *, heads,
           n_prompt_seq, n_prompt_chunks, n_sample_chunks):
    d = xp.shape[1]
    t = xp.shape[0] + xs.shape[0]
    nm = w_main.shape[1]
    w = heads * HEAD_DIM
    n_seq = sg0.shape[0]
    n_chunks = t // CHUNK
    n_prompt_total = n_prompt_seq * n_prompt_chunks
    rows_g = GROUP * CHUNK
    assert xp.shape[0] % rows_g == 0 and xs.shape[0] % rows_g == 0 and nm % (GROUP * PROJ_PIECES * LANES) == 0
    n_prompt_groups = xp.shape[0] // rows_g
    n_groups = t // rows_g
    tw = nm // GROUP

    def seq_of(s):
        c = jnp.maximum(s - GROUP, 0)
        return jnp.where(c < n_prompt_total, c // n_prompt_chunks,
                         n_prompt_seq + (c - n_prompt_total) // n_sample_chunks)

    const = lambda shape: pl.BlockSpec(shape, lambda s: (0,) * len(shape))
    st_spec = pl.BlockSpec((1, heads, HEAD_DIM, HEAD_DIM), lambda s: (seq_of(s), 0, 0, 0))
    cv_spec = pl.BlockSpec((1, 8, 3 * w), lambda s: (seq_of(s), 0, 0))
    kern = functools.partial(_front_kernel, heads, n_prompt_chunks, n_prompt_total, n_sample_chunks,
                             n_prompt_groups)
    return pl.pallas_call(
        kern,
        grid=(n_chunks + GROUP,),
        in_specs=_two_stream_specs(rows_g, d, n_prompt_groups,
                                   lambda s: jnp.minimum(s // GROUP, n_groups - 1)) + [
            const((1, d)),
            pl.BlockSpec((d, tw), lambda s: (0, s % GROUP)),
            const((d, LANES)),
            const((8, 3 * w)), const((8, LANES)), const((1, HEAD_DIM)), const((1, w)),
            const((1, HEAD_DIM)),
            st_spec, st_spec, cv_spec,
        ],
        out_specs=[
            pl.BlockSpec((CHUNK, 2 * w), lambda s: (jnp.maximum(s - GROUP, 0), 0)),
            st_spec, st_spec, cv_spec,
        ],
        out_shape=[
            jax.ShapeDtypeStruct((t, 2 * w), BF16),
            jax.ShapeDtypeStruct((n_seq, heads, HEAD_DIM, HEAD_DIM), F32),
            jax.ShapeDtypeStruct((n_seq, heads, HEAD_DIM, HEAD_DIM), F32),
            jax.ShapeDtypeStruct((n_seq, 8, 3 * w), F32),
        ],
        scratch_shapes=[
            pltpu.VMEM((rows_g, d), BF16),
            pltpu.VMEM((GROUP, rows_g, tw), F32), pltpu.VMEM((rows_g, LANES), F32),
            pltpu.VMEM((GROUP, rows_g, tw), F32), pltpu.VMEM((rows_g, LANES), F32),
            pltpu.VMEM((CHUNK + 8, 3 * w), F32), pltpu.VMEM((CHUNK, 3 * w), F32),
        ],
        compiler_params=_cparams(("arbitrary",)),
        name="front",
    )(xp, xs, g1, w_main, w_small, conv_w, ad, gng, lb, hng, sg0, sh0, cv0)


def _outproj_kernel(n_prompt_blocks, xp_ref, xs_ref, m_ref, wo_ref, g_ref, wr_ref, br_ref,
                    x1_ref, hp_ref, lg_ref):
    def body(x_ref):
        x1 = x_ref[...] + jnp.dot(m_ref[...], wo_ref[...], preferred_element_type=F32)
        x1_ref[...] = x1
        h2 = _rms(x1, g_ref[...])
        half = h2.shape[1] // 2
        bits = lax.bitcast_convert_type(h2.astype(BF16).astype(F32), U32)
        hp_ref[...] = (bits[:, :half] >> 16) | (bits[:, half:] & jnp.uint32(0xFFFF0000))
        wr = wr_ref[...]
        wr_hi = wr.astype(BF16)
        wr_lo = (wr - wr_hi.astype(F32)).astype(BF16)
        h_hi = h2.astype(BF16)
        h_lo = (h2 - h_hi.astype(F32)).astype(BF16)
        nt = lambda a, b: lax.dot_general(a, b, (((1,), (1,)), ((), ())), preferred_element_type=F32)
        lg_ref[...] = nt(wr_hi, h_hi) + nt(wr_hi, h_lo) + nt(wr_lo, h_hi) + br_ref[...]

    i = pl.program_id(0)
    pl.when(i < n_prompt_blocks)(lambda: body(xp_ref))
    pl.when(i >= n_prompt_blocks)(lambda: body(xs_ref))


def _outproj(xp, xs, mixed, w_out, g2, w_router_t, b_router_col):
    d = xp.shape[1]
    t = xp.shape[0] + xs.shape[0]
    e = w_router_t.shape[0]
    tm = _pick(math.gcd(xp.shape[0], xs.shape[0]), (512, 256, 128))
    n_prompt_blocks = xp.shape[0] // tm
    return pl.pallas_call(
        functools.partial(_outproj_kernel, n_prompt_blocks),
        grid=(t // tm,),
        in_specs=_two_stream_specs(tm, d, n_prompt_blocks, lambda i: i) + [
            pl.BlockSpec((tm, d), lambda i: (i, 0)),
            pl.BlockSpec((d, d), lambda i: (0, 0), pipeline_mode=pl.Buffered(1)),
            pl.BlockSpec((1, d), lambda i: (0, 0)),
            pl.BlockSpec((e, d), lambda i: (0, 0)),
            pl.BlockSpec((e, 1), lambda i: (0, 0)),
        ],
        out_specs=[
            pl.BlockSpec((tm, d), lambda i: (i, 0)),
            pl.BlockSpec((tm, d // 2), lambda i: (i, 0)),
            pl.BlockSpec((e, tm), lambda i: (0, i)),
        ],
        out_shape=[
            jax.ShapeDtypeStruct((t, d), F32),
            jax.ShapeDtypeStruct((t, d // 2), U32),
            jax.ShapeDtypeStruct((e, t), F32),
        ],
        compiler_params=_cparams(("arbitrary",)),
        name="outproj",
    )(xp, xs, mixed, w_out, g2, w_router_t, b_router_col)


def _route_kernel(lg_ref, tri_ref, e_ref, gate_ref, rank_ref, cnt_ref, carry):
    @pl.when(pl.program_id(0) == 0)
    def _():
        carry[...] = jnp.zeros_like(carry)

    lg = lg_ref[...]
    n_exp, tb = lg.shape
    eio = lax.broadcasted_iota(I32, (n_exp, tb), 0).astype(F32)
    work = lg
    vals, onehots = [], []
    for k in range(TOP_K):
        m = jnp.max(work, axis=0, keepdims=True)
        idx = jnp.min(jnp.where(work == m, eio, float(n_exp)), axis=0, keepdims=True)
        oh = eio == idx
        e_ref[k:k + 1, :] = idx.astype(I32)
        vals.append(m)
        onehots.append(oh)
        work = jnp.where(oh, -jnp.inf, work)
    exps = [jnp.exp(v - vals[0]) for v in vals]
    denom = exps[0] + exps[1] + exps[2] + exps[3]
    for k in range(TOP_K):
        gate_ref[k:k + 1, :] = exps[k] / denom
    self32 = sum(oh.astype(F32) for oh in onehots)
    before = carry[...] + jnp.dot(self32.astype(BF16), tri_ref[...], preferred_element_type=F32)
    for k in range(TOP_K):
        rank_ref[k:k + 1, :] = jnp.sum(jnp.where(onehots[k], before, 0.0), axis=0,
                                       keepdims=True).astype(I32)
    pad = jnp.zeros((8 - TOP_K, tb), I32)
    e_ref[TOP_K:8, :] = pad
    rank_ref[TOP_K:8, :] = pad
    gate_ref[TOP_K:8, :] = pad.astype(F32)
    carry[...] = carry[...] + jnp.sum(self32, axis=1, keepdims=True)
    cnt_ref[...] = jnp.broadcast_to(carry[...], cnt_ref.shape)


def _route(logits_t):
    n_exp, t = logits_t.shape
    tb = _pick(t, (512, 256, 128))
    r = lax.broadcasted_iota(I32, (tb, tb), 0)
    c = lax.broadcasted_iota(I32, (tb, tb), 1)
    tri = (r < c).astype(BF16)
    row_spec = pl.BlockSpec((8, tb), lambda i: (0, i))
    return pl.pallas_call(
        _route_kernel,
        grid=(t // tb,),
        in_specs=[pl.BlockSpec((n_exp, tb), lambda i: (0, i)),
                  pl.BlockSpec((tb, tb), lambda i: (0, 0))],
        out_specs=[row_spec, row_spec, row_spec, pl.BlockSpec((n_exp, LANES), lambda i: (0, 0))],
        out_shape=[
            jax.ShapeDtypeStruct((8, t), I32),
            jax.ShapeDtypeStruct((8, t), F32),
            jax.ShapeDtypeStruct((8, t), I32),
            jax.ShapeDtypeStruct((n_exp, LANES), F32),
        ],
        scratch_shapes=[pltpu.VMEM((n_exp, 1), F32)],
        compiler_params=_cparams(("arbitrary",)),
        name="route",
    )(logits_t, tri)


def _dispatch_kernel(pad_start_ref, pad_len_ref, tail_ref, dest_ref, hp_ref, xs_ref, zbuf, sem, zsem):
    tb = hp_ref.shape[0]
    zrows = zbuf.shape[0]

    @pl.when(pl.program_id(0) == 0)
    def _():
        zbuf[...] = jnp.zeros_like(zbuf)

        def pad_copy(r):
            return pltpu.make_async_copy(zbuf.at[pl.ds(0, 1), :], xs_ref.at[pl.ds(r, 1), :], zsem)

        def tail_copy(b):
            r = pl.multiple_of(tail_ref[0] + b * zrows, zrows)
            return pltpu.make_async_copy(zbuf, xs_ref.at[pl.ds(r, zrows), :], zsem)

        def for_pads(fn):
            def per_expert(e, carry):
                lax.fori_loop(0, pad_len_ref[e], lambda r, c: (fn(pad_start_ref[e] + r), c)[1], 0)
                return carry
            lax.fori_loop(0, pad_start_ref.shape[0], per_expert, 0)

        def for_tail(fn):
            lax.fori_loop(0, tail_ref[1], lambda b, c: (fn(b), c)[1], 0)

        for_pads(lambda r: pad_copy(r).start())
        for_tail(lambda b: tail_copy(b).start())
        for_pads(lambda r: pad_copy(0).wait())
        for_tail(lambda b: tail_copy(0).wait())

    def row_copy(t, d):
        return pltpu.make_async_copy(hp_ref.at[pl.ds(t, 1), :], xs_ref.at[pl.ds(d, 1), :], sem)

    def issue(t, carry):
        for k in range(TOP_K):
            row_copy(t, dest_ref[0, k, t]).start(priority=k % 2)
        return carry

    lax.fori_loop(0, tb, issue, 0)

    def drain(t, carry):
        for k in range(TOP_K):
            row_copy(0, 0).wait()
        return carry

    lax.fori_loop(0, tb, drain, 0)


def _dispatch(pad_start, pad_len, tail, dest3, hp, n_rows, zrows):
    t, half = hp.shape
    tb = dest3.shape[2]
    grid_spec = pltpu.PrefetchScalarGridSpec(
        num_scalar_prefetch=3,
        grid=(t // tb,),
        in_specs=[
            pl.BlockSpec((1, TOP_K, tb), lambda i, *_: (i, 0, 0), memory_space=pltpu.SMEM),
            pl.BlockSpec((tb, half), lambda i, *_: (i, 0)),
        ],
        out_specs=pl.BlockSpec(memory_space=pl.ANY),
        scratch_shapes=[pltpu.VMEM((zrows, half), U32), pltpu.SemaphoreType.DMA(()),
                        pltpu.SemaphoreType.DMA(())],
    )
    return pl.pallas_call(
        _dispatch_kernel,
        grid_spec=grid_spec,
        out_shape=jax.ShapeDtypeStruct((n_rows, half), U32),
        compiler_params=_cparams(("arbitrary",)),
        name="dispatch",
    )(pad_start, pad_len, tail, dest3, hp)


def _expert_kernel(nf, te_ref, nu_ref, xs_ref, wg_ref, wu_ref, bg_ref, bu_ref, wd_ref, bd_ref,
                   out_ref, acc, x_scr):
    del te_ref
    i = pl.program_id(0)
    j = pl.program_id(1)

    @pl.when(i < nu_ref[0])
    def _():
        @pl.when(j == 0)
        def _():
            words = xs_ref[...]
            half = words.shape[1]
            x_scr[:, :half] = lax.bitcast_convert_type(words << 16, F32).astype(BF16)
            x_scr[:, half:] = lax.bitcast_convert_type(words & jnp.uint32(0xFFFF0000), F32).astype(BF16)

        x = x_scr[...]
        gate = jnp.dot(x, wg_ref[0], preferred_element_type=F32) + bg_ref[0]
        up = jnp.dot(x, wu_ref[0], preferred_element_type=F32) + bu_ref[0]
        gate = jnp.minimum(gate, SWIGLU_LIMIT)
        up = jnp.clip(up, -SWIGLU_LIMIT, SWIGLU_LIMIT)
        act = ((up + 1.0) * gate * _sigmoid(SWIGLU_ALPHA * gate)).astype(BF16)
        part = jnp.dot(act, wd_ref[0], preferred_element_type=F32)
        if nf == 1:
            out_ref[...] = part + bd_ref[0]
        else:
            @pl.when(j == 0)
            def _():
                acc[...] = part + bd_ref[0]

            @pl.when((j > 0) & (j < nf - 1))
            def _():
                acc[...] += part

            @pl.when(j == nf - 1)
            def _():
                out_ref[...] = acc[...] + part

    @pl.when((i >= nu_ref[0]) & (j == nf - 1))
    def _():
        out_ref[...] = jnp.zeros_like(out_ref)


def _experts(tile_expert, n_used, xs, w_gate_up, b_gate_up3, w_down, b_down3, *, tm):
    n_rows, half = xs.shape
    d = 2 * half
    n_exp, _, two_f = w_gate_up.shape
    f = two_f // 2
    tf = _pick(f, (1024, 512, 256, 128))
    nf = f // tf
    n_tiles = n_rows // tm

    def tile(i, nu):
        return jnp.minimum(i, nu[0] - 1)

    grid_spec = pltpu.PrefetchScalarGridSpec(
        num_scalar_prefetch=2,
        grid=(n_tiles, nf),
        in_specs=[
            pl.BlockSpec((tm, half), lambda i, j, te, nu: (tile(i, nu), 0)),
            pl.BlockSpec((1, d, tf), lambda i, j, te, nu: (te[tile(i, nu)], 0, j)),
            pl.BlockSpec((1, d, tf), lambda i, j, te, nu: (te[tile(i, nu)], 0, nf + j)),
            pl.BlockSpec((1, 1, tf), lambda i, j, te, nu: (te[tile(i, nu)], 0, j)),
            pl.BlockSpec((1, 1, tf), lambda i, j, te, nu: (te[tile(i, nu)], 0, nf + j)),
            pl.BlockSpec((1, tf, d), lambda i, j, te, nu: (te[tile(i, nu)], j, 0)),
            pl.BlockSpec((1, 1, d), lambda i, j, te, nu: (te[tile(i, nu)], 0, 0)),
        ],
        out_specs=pl.BlockSpec((tm, d), lambda i, j, te, nu: (i, 0)),
        scratch_shapes=[pltpu.VMEM((tm, d), F32), pltpu.VMEM((tm, d), BF16)],
    )
    return pl.pallas_call(
        functools.partial(_expert_kernel, nf),
        grid_spec=grid_spec,
        out_shape=jax.ShapeDtypeStruct((n_rows, d), F32),
        compiler_params=_cparams(("arbitrary", "arbitrary")),
        name="experts",
    )(tile_expert, n_used, xs, w_gate_up, w_gate_up, b_gate_up3, b_gate_up3, w_down, b_down3)


def _combine_kernel(n, dest_ref, dest_next_ref, x1_ref, gates_ref, gf_ref, ys_ref, y_ref, gbuf, sems):
    i = pl.program_id(0)
    tb = x1_ref.shape[0]

    def row_copy(slot, k, t, d):
        return pltpu.make_async_copy(ys_ref.at[pl.ds(d, 1), :], gbuf.at[slot, k, pl.ds(t, 1), :],
                                     sems.at[slot])

    def gather(d_ref, slot):
        def issue(t, carry):
            for k in range(TOP_K):
                row_copy(slot, k, t, d_ref[0, k, t]).start(priority=k % 2)
            return carry
        lax.fori_loop(0, tb, issue, 0)

    pl.when(i == 0)(lambda: gather(dest_ref, 0))
    pl.when(i + 1 < n)(lambda: gather(dest_next_ref, (i + 1) % 2))

    slot = i % 2

    def drain(t, carry):
        for k in range(TOP_K):
            row_copy(slot, 0, 0, 0).wait()
        return carry

    lax.fori_loop(0, tb, drain, 0)

    gates = gates_ref[...]
    y = x1_ref[...]
    for k in range(TOP_K):
        y = y + gates[:, k:k + 1] * gbuf[slot, k]
    y_ref[...] = _rms(y, gf_ref[...])


def _combine(dest3, x1, gates_tk, gf, ys, blk0, n_blk):
    d = x1.shape[1]
    tb = dest3.shape[2]
    return pl.pallas_call(
        functools.partial(_combine_kernel, n_blk),
        grid=(n_blk,),
        in_specs=[
            pl.BlockSpec((1, TOP_K, tb), lambda i: (blk0 + i, 0, 0), memory_space=pltpu.SMEM),
            pl.BlockSpec((1, TOP_K, tb), lambda i: (blk0 + jnp.minimum(i + 1, n_blk - 1), 0, 0),
                         memory_space=pltpu.SMEM),
            pl.BlockSpec((tb, d), lambda i: (blk0 + i, 0)),
            pl.BlockSpec((tb, TOP_K), lambda i: (blk0 + i, 0)),
            pl.BlockSpec((1, d), lambda i: (0, 0)),
            pl.BlockSpec(memory_space=pl.ANY),
        ],
        out_specs=pl.BlockSpec((tb, d), lambda i: (i, 0)),
        out_shape=jax.ShapeDtypeStruct((n_blk * tb, d), F32),
        scratch_shapes=[pltpu.VMEM((2, TOP_K, tb, d), F32), pltpu.SemaphoreType.DMA((2,))],
        compiler_params=_cparams(("arbitrary",)),
        name="combine",
    )(dest3, dest3, x1, gates_tk, gf, ys)


def kernel(x_prompt, x_sample, state_gdn, cache_gdn_conv, state_hgrn, ln1_g, w_in, gdn_conv_w,
           gdn_A_log, gdn_dt_bias, gdn_norm_g, hgrn_lb_logits, hgrn_norm_g, w_out, ln2_g,
           w_router, b_router, w_gate_up, b_gate_up, w_down, b_down, ln_f_g):
    nb, seq, d = x_prompt.shape
    db, dseq, _ = x_sample.shape
    depth, _, heads, dk, dv = state_gdn.shape
    assert depth == 1 and dk == HEAD_DIM and dv == HEAD_DIM
    assert state_hgrn.shape[2:] == (heads, HEAD_DIM, HEAD_DIM)
    assert seq % CHUNK == 0 and dseq % CHUNK == 0 and 2 * heads <= LANES
    w = heads * HEAD_DIM
    assert d == 2 * w and w_in.shape[2] == 8 * w + 2 * heads
    n_exp = w_router.shape[2]
    t_p = nb * seq
    t_all = t_p + db * dseq
    n_seq = nb + db

    wi = w_in[0]
    w_main = jnp.concatenate([wi[:, :4 * w], wi[:, 4 * w + 2 * heads:]], axis=1).astype(BF16)
    w_small = jnp.pad(wi[:, 4 * w:4 * w + 2 * heads], ((0, 0), (0, LANES - 2 * heads))).astype(BF16)
    conv_w8 = jnp.pad(gdn_conv_w[0], ((0, 8 - CONV_W), (0, 0)))
    lane_pad = lambda v: jnp.pad(v, (heads, LANES - 2 * heads))
    ad = jnp.pad(jnp.stack([lane_pad(gdn_A_log[0]), lane_pad(gdn_dt_bias[0])]), ((0, 6), (0, 0)))
    lower_bounds = jnp.cumsum(jax.nn.softmax(hgrn_lb_logits.astype(F32), axis=0), axis=0)[0:1]
    zeros_state = jnp.zeros((nb, heads, HEAD_DIM, HEAD_DIM), F32)
    sg0 = jnp.concatenate([zeros_state, state_gdn[0]], axis=0)
    sh0 = jnp.concatenate([zeros_state, state_hgrn[0]], axis=0)
    cv0 = jnp.pad(jnp.concatenate([jnp.zeros((nb, CONV_W - 1, 3 * w), F32), cache_gdn_conv[0]], axis=0),
                  ((0, 0), (8 - (CONV_W - 1), 0), (0, 0)))

    xp = x_prompt.reshape(t_p, d)
    xs_tok = x_sample.reshape(db * dseq, d)

    mixed, sg, sh, cv = _front(xp, xs_tok, ln1_g, w_main, w_small, conv_w8, ad, gdn_norm_g,
                               lower_bounds, hgrn_norm_g, sg0, sh0, cv0, heads=heads,
                               n_prompt_seq=nb, n_prompt_chunks=seq // CHUNK,
                               n_sample_chunks=dseq // CHUNK)
    x1, hp, logits_t = _outproj(xp, xs_tok, mixed, w_out[0].astype(BF16), ln2_g,
                                w_router[0].T, b_router[0][:, None])

    e_idx, gates, ranks, counts = _route(logits_t)
    tm_e = 512 if t_all * TOP_K >= 512 * n_exp else 128
    cnt = counts[:, 0].astype(I32)
    padded = (cnt + tm_e - 1) // tm_e * tm_e
    seg_end = jnp.cumsum(padded)
    seg_start = seg_end - padded
    n_tiles = (t_all * TOP_K + n_exp * (tm_e - 1)) // tm_e
    onehot = e_idx[:TOP_K, :, None] == jnp.arange(n_exp, dtype=I32)
    dest = jnp.sum(jnp.where(onehot, seg_start, 0), axis=-1) + ranks[:TOP_K]
    tb = _pick(math.gcd(t_p, t_all - t_p), (256, 128))
    dest3 = dest.reshape(TOP_K, t_all // tb, tb).transpose(1, 0, 2)
    tile_start = jnp.arange(n_tiles, dtype=I32) * tm_e
    tile_expert = jnp.minimum(jnp.sum(seg_end[None, :] <= tile_start[:, None], axis=1),
                              n_exp - 1).astype(I32)
    n_used = (seg_end[-1:] // tm_e).astype(I32)

    zrows = 64
    tail = jnp.stack([seg_end[-1], (n_tiles * tm_e - seg_end[-1]) // zrows]).astype(I32)
    xs = _dispatch(seg_start + cnt, padded - cnt, tail, dest3, hp, n_tiles * tm_e, zrows)
    ys = _experts(tile_expert, n_used, xs, w_gate_up[0].astype(BF16), b_gate_up[0][:, None, :],
                  w_down[0].astype(BF16), b_down[0][:, None, :], tm=tm_e)
    gates_tk = gates[:TOP_K].T
    gf = ln_f_g[None, :]
    y_prompt = _combine(dest3, x1, gates_tk, gf, ys, 0, t_p // tb).reshape(nb, seq, d)
    y_sample = _combine(dest3, x1, gates_tk, gf, ys, t_p // tb, (t_all - t_p) // tb).reshape(db, dseq, d)
    cache = cv[:, 8 - (CONV_W - 1):, :]
    return (y_prompt, y_sample,
            sg[None, :nb], cache[None, :nb], sh[None, :nb],
            sg[None, nb:], cache[None, nb:], sh[None, nb:])
```

```python
import functools
import math

import jax
import jax.numpy as jnp
from jax import lax
from jax.experimental import pallas as pl
from jax.experimental.pallas import tpu as pltpu

F32 = jnp.float32
BF16 = jnp.bfloat16
U32 = jnp.uint32
I32 = jnp.int32

CHUNK = 64
CONV_W = 4
TOP_K = 4
NORM_EPS = 1e-6
L2_EPS = 1e-6
SWIGLU_LIMIT = 7.0
SWIGLU_ALPHA = 1.702
HEAD_DIM = 128
LANES = 128
HGRN_SUB = 16
ROW_GROUP = 8
VMEM_LIMIT = 56 * 1024 * 1024


def _pick(n, cands):
    for c in cands:
        if n % c == 0:
            return c
    raise ValueError(f"no tile in {cands} divides {n}")


def _cparams(sem):
    return pltpu.CompilerParams(dimension_semantics=sem, vmem_limit_bytes=VMEM_LIMIT)


def _sigmoid(x):
    return 1.0 / (1.0 + jnp.exp(-x))


def _silu(x):
    return x * _sigmoid(x)


def _softplus(x):
    return jnp.maximum(x, 0.0) + jnp.log1p(jnp.exp(-jnp.abs(x)))


def _rms(x, g):
    return x * lax.rsqrt(jnp.mean(x * x, axis=-1, keepdims=True) + NORM_EPS) * g


def _dot(a, b):
    return jnp.dot(a.astype(BF16), b.astype(BF16), preferred_element_type=F32)


def _dot_nt(a, b):
    return lax.dot_general(a.astype(BF16), b.astype(BF16), (((1,), (1,)), ((), ())),
                           preferred_element_type=F32)


def _dot_tn(a, b):
    return lax.dot_general(a.astype(BF16), b.astype(BF16), (((0,), (0,)), ((), ())),
                           preferred_element_type=F32)


GROUP = 4
PROJ_PIECES = 4


def _two_stream_specs(tm, d, n_prompt_blocks, block_of):
    return [
        pl.BlockSpec((tm, d), lambda s: (jnp.minimum(block_of(s), n_prompt_blocks - 1), 0)),
        pl.BlockSpec((tm, d), lambda s: (jnp.maximum(block_of(s) - n_prompt_blocks, 0), 0)),
    ]


def _cumsum_rows(x, row_idx):
    s = 1
    while s < x.shape[0]:
        x = x + jnp.where(row_idx >= s, pltpu.roll(x, s, 0), 0.0)
        s *= 2
    return x


def _front_kernel(heads, n_prompt_chunks, n_prompt_total, n_sample_chunks, n_prompt_groups,
                  xp_ref, xs_ref, g1_ref, wm_ref, ws_ref,
                  cw_ref, ad_ref, gng_ref, lb_ref, hng_ref, sg0_ref, sh0_ref, cv0_ref,
                  mix_ref, sg_ref, sh_ref, cv_ref,
                  h_scr, pm_a, ps_a, pm_b, ps_b, cbuf, qkv_scr):
    s = pl.program_id(0)
    g = s // GROUP

    @pl.when(s == 0)
    def _():
        for buf in (pm_a, ps_a, pm_b, ps_b):
            buf[...] = jnp.zeros_like(buf)

    args = (heads, n_prompt_chunks, n_prompt_total, n_sample_chunks, n_prompt_groups,
            xp_ref, xs_ref, g1_ref, wm_ref, ws_ref,
            cw_ref, ad_ref, gng_ref, lb_ref, hng_ref, sg0_ref, sh0_ref, cv0_ref,
            mix_ref, sg_ref, sh_ref, cv_ref, h_scr, cbuf, qkv_scr)
    pl.when(g % 2 == 0)(lambda: _front_step(*args, pm_a, ps_a, pm_b, ps_b))
    pl.when(g % 2 == 1)(lambda: _front_step(*args, pm_b, ps_b, pm_a, ps_a))


def _front_step(heads, n_prompt_chunks, n_prompt_total, n_sample_chunks, n_prompt_groups,
                xp_ref, xs_ref, g1_ref, wm_ref, ws_ref,
                cw_ref, ad_ref, gng_ref, lb_ref, hng_ref, sg0_ref, sh0_ref, cv0_ref,
                mix_ref, sg_ref, sh_ref, cv_ref, h_scr, cbuf, qkv_scr,
                pm_w, ps_w, pm_r, ps_r):
    s = pl.program_id(0)
    w = heads * HEAD_DIM
    L = CHUNK
    tw = wm_ref.shape[1]
    pw = tw // PROJ_PIECES
    g = s // GROUP
    j = s % GROUP
    c = jnp.maximum(s - GROUP, 0)
    r0 = pl.multiple_of((c % GROUP) * L, L)
    is_start = jnp.where(c < n_prompt_total, (c % n_prompt_chunks) == 0,
                         ((c - n_prompt_total) % n_sample_chunks) == 0)

    def norm_from(x_ref):
        hb = _rms(x_ref[...], g1_ref[...]).astype(BF16)
        h_scr[...] = hb
        ps_w[...] = jnp.dot(hb, ws_ref[...], preferred_element_type=F32)

    pl.when((j == 0) & (g < n_prompt_groups))(lambda: norm_from(xp_ref))
    pl.when((j == 0) & (g >= n_prompt_groups))(lambda: norm_from(xs_ref))

    def proj_piece(p):
        cols = slice(p * pw, (p + 1) * pw)
        pm_w[j, :, cols] = jnp.dot(h_scr[...], wm_ref[:, cols], preferred_element_type=F32)

    def pm(col0, width=HEAD_DIM):
        tile, off = divmod(col0, tw)
        assert off + width <= tw
        return pm_r[tile, pl.ds(r0, L), off:off + width]

    @pl.when(is_start)
    def _():
        sg_ref[...] = sg0_ref[...]
        sh_ref[...] = sh0_ref[...]
        cbuf[0:8, :] = cv0_ref[0]

    proj_piece(0)
    proj_piece(1)
    proj_piece(2)

    for c0 in range(0, 3 * w, tw):
        c1 = min(c0 + tw, 3 * w)
        cbuf[8:8 + L, c0:c1] = pm(c0, c1 - c0)
    conv = cbuf[5:5 + L, :] * cw_ref[0:1, :]
    for tap in range(1, CONV_W):
        conv = conv + cbuf[5 + tap:5 + tap + L, :] * cw_ref[tap:tap + 1, :]
    tail = cbuf[L:L + 8, :]
    cv_ref[0] = tail
    cbuf[0:8, :] = tail
    qkv_scr[...] = _silu(conv)

    row = lax.broadcasted_iota(I32, (L, L), 0)
    col = lax.broadcasted_iota(I32, (L, L), 1)
    incl = row >= col
    strict = row > col
    eye = (row == col).astype(F32)
    lvl_masks = []
    for lg2 in range(L.bit_length() - 1):
        same_pair = (row >> (lg2 + 1)) == (col >> (lg2 + 1))
        lower_left = (((row >> lg2) & 1) == 1) & (((col >> lg2) & 1) == 0)
        lvl_masks.append(same_pair & lower_left)
    row_l = lax.broadcasted_iota(I32, (L, HEAD_DIM), 0)
    sub_row = lax.broadcasted_iota(I32, (HGRN_SUB, L), 0)
    sub_col = lax.broadcasted_iota(I32, (HGRN_SUB, L), 1)
    hs = range(heads)
    lane = lambda base, h: slice(base + h * HEAD_DIM, base + (h + 1) * HEAD_DIM)

    def gated_norm(o, g_ref, z):
        return o * lax.rsqrt(jnp.mean(o * o, axis=-1, keepdims=True) + NORM_EPS) * g_ref[...] * _silu(z)

    def hgrn_front(h):
        qh = _silu(pm(4 * w + h * HEAD_DIM))
        lbh = lb_ref[:, lane(0, h)]
        f = lbh + (1.0 - lbh) * _sigmoid(pm(5 * w + h * HEAD_DIM))
        kh = 1.0 - f
        v = pm(6 * w + h * HEAD_DIM)
        bc = _cumsum_rows(jnp.log(f), row_l)
        a_parts = []
        for i in range(L // HGRN_SUB):
            r0 = i * HGRN_SUB
            r1 = r0 + HGRN_SUB
            e_i = bc[r0 - 1:r0, :] if i > 0 else jnp.zeros((1, HEAD_DIM), F32)
            qt = qh[r0:r1] * jnp.exp(bc[r0:r1] - e_i)
            kt = kh * jnp.exp(jnp.where(row_l < r1, e_i - bc, 0.0))
            a_parts.append(jnp.where(sub_col <= sub_row + r0, _dot_nt(qt, kt), 0.0))
        s = sh_ref[0, h]
        o_inter = _dot(qh * jnp.exp(bc), s)
        ft_col = jnp.exp(bc.T[:, L - 1:L])
        sh_ref[0, h] = ft_col * s + _dot_tn(kh * jnp.exp(bc[L - 1:L, :] - bc), v)
        return jnp.concatenate(a_parts, axis=0), v, o_inter

    def hgrn_back(h, front):
        attn_h, v, o_inter = front
        o = gated_norm(o_inter + _dot(attn_h, v), hng_ref, pm(7 * w + h * HEAD_DIM))
        mix_ref[:, lane(w, h)] = o.astype(mix_ref.dtype)

    fronts = {}

    def hgrn_slot(i):
        if i < heads:
            fronts[i] = hgrn_front(i)
        if 0 <= i - 1 < heads:
            hgrn_back(i - 1, fronts.pop(i - 1))

    ps = ps_r[pl.ds(r0, L), :]
    beta_all = _sigmoid(ps)
    g_all = -jnp.exp(ad_ref[0:1, :]) * _softplus(ps + ad_ref[1:2, :])
    gc_all = _cumsum_rows(g_all, row_l)
    gc_t = gc_all.T
    q, k, v, bcol, gcol, egc, glast, dec_incl = ([] for _ in range(8))
    for h in hs:
        q_c = qkv_scr[:, lane(0, h)]
        k_c = qkv_scr[:, lane(w, h)]
        v.append(qkv_scr[:, lane(2 * w, h)])
        q.append(q_c * lax.rsqrt(jnp.sum(q_c * q_c, axis=-1, keepdims=True) + L2_EPS) * (HEAD_DIM ** -0.5))
        k.append(k_c * lax.rsqrt(jnp.sum(k_c * k_c, axis=-1, keepdims=True) + L2_EPS))
        bcol.append(beta_all[:, h:h + 1])
        gcol.append(gc_all[:, heads + h:heads + h + 1])
        glast.append(gc_all[L - 1:L, heads + h:heads + h + 1])
        egc.append(jnp.exp(gcol[h]))
        grow = gc_t[heads + h:heads + h + 1, :]
        dec_incl.append(jnp.where(incl, jnp.exp(jnp.where(incl, gcol[h] - grow, 0.0)), 0.0))
    kk = [_dot_nt(k[h], k[h]) for h in hs]
    qk = [_dot_nt(q[h], k[h]) for h in hs]
    a_mat = [bcol[h] * kk[h] * jnp.where(strict, dec_incl[h], 0.0) for h in hs]
    attn = [qk[h] * dec_incl[h] for h in hs]
    s_old = [sg_ref[0, h] for h in hs]
    o_inter = [_dot(q[h] * egc[h], s_old[h]) for h in hs]
    x = [eye - jnp.where(lvl_masks[0], a_mat[h], 0.0) for h in hs]
    piece_after_level = {1: 3}
    assert PROJ_PIECES == 4 and len(lvl_masks) > 4
    slot = 0
    for lvl, m in enumerate(lvl_masks[1:]):
        y = [_dot(x[h], jnp.where(m, a_mat[h], 0.0)) for h in hs]
        hgrn_slot(slot)
        x = [x[h] - _dot(y[h], x[h]) for h in hs]
        hgrn_slot(slot + 1)
        slot += 2
        if lvl in piece_after_level:
            proj_piece(piece_after_level[lvl])
    sol = [_dot(x[h], jnp.concatenate([bcol[h] * v[h], (bcol[h] * egc[h]) * k[h]], axis=-1)) for h in hs]
    while slot <= heads:
        hgrn_slot(slot)
        slot += 1
    delta = [sol[h][:, :HEAD_DIM] - _dot(sol[h][:, HEAD_DIM:], s_old[h]) for h in hs]
    o_intra = [_dot(attn[h], delta[h]) for h in hs]
    s_add = [_dot_tn(k[h] * jnp.exp(glast[h] - gcol[h]), delta[h]) for h in hs]
    for h in hs:
        sg_ref[0, h] = jnp.exp(glast[h]) * s_old[h] + s_add[h]
        o = gated_norm(o_inter[h] + o_intra[h], gng_ref, pm(3 * w + h * HEAD_DIM))
        mix_ref[:, lane(0, h)] = o.astype(mix_ref.dtype)


def _front(xp, xs, g1, w_main, w_small, conv_w, ad, gng, lb, hng, sg0, sh0, cv0, *, heads,
           n_prompt_seq, n_prompt_chunks, n_sample_chunks):
    d = xp.shape[1]
    t = xp.shape[0] + xs.shape[0]
    nm = w_main.shape[1]
    w = heads * HEAD_DIM
    n_seq = sg0.shape[0]
    n_chunks = t // CHUNK
    n_prompt_total = n_prompt_seq * n_prompt_chunks
    rows_g = GROUP * CHUNK
    assert xp.shape[0] % rows_g == 0 and xs.shape[0] % rows_g == 0 and nm % (GROUP * PROJ_PIECES * LANES) == 0
    n_prompt_groups = xp.shape[0] // rows_g
    n_groups = t // rows_g
    tw = nm // GROUP

    def seq_of(s):
        c = jnp.maximum(s - GROUP, 0)
        return jnp.where(c < n_prompt_total, c // n_prompt_chunks,
                         n_prompt_seq + (c - n_prompt_total) // n_sample_chunks)

    const = lambda shape: pl.BlockSpec(shape, lambda s: (0,) * len(shape))
    st_spec = pl.BlockSpec((1, heads, HEAD_DIM, HEAD_DIM), lambda s: (seq_of(s), 0, 0, 0))
    cv_spec = pl.BlockSpec((1, 8, 3 * w), lambda s: (seq_of(s), 0, 0))
    kern = functools.partial(_front_kernel, heads, n_prompt_chunks, n_prompt_total, n_sample_chunks,
                             n_prompt_groups)
    return pl.pallas_call(
        kern,
        grid=(n_chunks + GROUP,),
        in_specs=_two_stream_specs(rows_g, d, n_prompt_groups,
                                   lambda s: jnp.minimum(s // GROUP, n_groups - 1)) + [
            const((1, d)),
            pl.BlockSpec((d, tw), lambda s: (0, s % GROUP)),
            const((d, LANES)),
            const((8, 3 * w)), const((8, LANES)), const((1, HEAD_DIM)), const((1, w)),
            const((1, HEAD_DIM)),
            st_spec, st_spec, cv_spec,
        ],
        out_specs=[
            pl.BlockSpec((CHUNK, 2 * w), lambda s: (jnp.maximum(s - GROUP, 0), 0)),
            st_spec, st_spec, cv_spec,
        ],
        out_shape=[
            jax.ShapeDtypeStruct((t, 2 * w), BF16),
            jax.ShapeDtypeStruct((n_seq, heads, HEAD_DIM, HEAD_DIM), F32),
            jax.ShapeDtypeStruct((n_seq, heads, HEAD_DIM, HEAD_DIM), F32),
            jax.ShapeDtypeStruct((n_seq, 8, 3 * w), F32),
        ],
        scratch_shapes=[
            pltpu.VMEM((rows_g, d), BF16),
            pltpu.VMEM((GROUP, rows_g, tw), F32), pltpu.VMEM((rows_g, LANES), F32),
            pltpu.VMEM((GROUP, rows_g, tw), F32), pltpu.VMEM((rows_g, LANES), F32),
            pltpu.VMEM((CHUNK + 8, 3 * w), F32), pltpu.VMEM((CHUNK, 3 * w), F32),
        ],
        compiler_params=_cparams(("arbitrary",)),
        name="front",
    )(xp, xs, g1, w_main, w_small, conv_w, ad, gng, lb, hng, sg0, sh0, cv0)


def _outproj_kernel(n_prompt_blocks, xp_ref, xs_ref, m_ref, wo_ref, g_ref, wr_ref, br_ref,
                    x1_ref, hp_ref, lg_ref):
    def body(x_ref):
        x1 = x_ref[...] + jnp.dot(m_ref[...], wo_ref[...], preferred_element_type=F32)
        x1_ref[...] = x1
        h2 = _rms(x1, g_ref[...])
        half = h2.shape[1] // 2
        bits = lax.bitcast_convert_type(h2.astype(BF16).astype(F32), U32)
        hp_ref[...] = (bits[:, :half] >> 16) | (bits[:, half:] & jnp.uint32(0xFFFF0000))
        wr = wr_ref[...]
        wr_hi = wr.astype(BF16)
        wr_lo = (wr - wr_hi.astype(F32)).astype(BF16)
        h_hi = h2.astype(BF16)
        h_lo = (h2 - h_hi.astype(F32)).astype(BF16)
        nt = lambda a, b: lax.dot_general(a, b, (((1,), (1,)), ((), ())), preferred_element_type=F32)
        lg_ref[...] = nt(wr_hi, h_hi) + nt(wr_hi, h_lo) + nt(wr_lo, h_hi) + br_ref[...]

    i = pl.program_id(0)
    pl.when(i < n_prompt_blocks)(lambda: body(xp_ref))
    pl.when(i >= n_prompt_blocks)(lambda: body(xs_ref))


def _outproj(xp, xs, mixed, w_out, g2, w_router_t, b_router_col):
    d = xp.shape[1]
    t = xp.shape[0] + xs.shape[0]
    e = w_router_t.shape[0]
    tm = _pick(math.gcd(xp.shape[0], xs.shape[0]), (512, 256, 128))
    n_prompt_blocks = xp.shape[0] // tm
    return pl.pallas_call(
        functools.partial(_outproj_kernel, n_prompt_blocks),
        grid=(t // tm,),
        in_specs=_two_stream_specs(tm, d, n_prompt_blocks, lambda i: i) + [
            pl.BlockSpec((tm, d), lambda i: (i, 0)),
            pl.BlockSpec((d, d), lambda i: (0, 0), pipeline_mode=pl.Buffered(1)),
            pl.BlockSpec((1, d), lambda i: (0, 0)),
            pl.BlockSpec((e, d), lambda i: (0, 0)),
            pl.BlockSpec((e, 1), lambda i: (0, 0)),
        ],
        out_specs=[
            pl.BlockSpec((tm, d), lambda i: (i, 0)),
            pl.BlockSpec((tm, d // 2), lambda i: (i, 0)),
            pl.BlockSpec((e, tm), lambda i: (0, i)),
        ],
        out_shape=[
            jax.ShapeDtypeStruct((t, d), F32),
            jax.ShapeDtypeStruct((t, d // 2), U32),
            jax.ShapeDtypeStruct((e, t), F32),
        ],
        compiler_params=_cparams(("arbitrary",)),
        name="outproj",
    )(xp, xs, mixed, w_out, g2, w_router_t, b_router_col)


def _route_kernel(lg_ref, tri_ref, e_ref, gate_ref, rank_ref, cnt_ref, carry):
    @pl.when(pl.program_id(0) == 0)
    def _():
        carry[...] = jnp.zeros_like(carry)

    lg = lg_ref[...]
    n_exp, tb = lg.shape
    eio = lax.broadcasted_iota(I32, (n_exp, tb), 0).astype(F32)
    work = lg
    vals, onehots = [], []
    for k in range(TOP_K):
        m = jnp.max(work, axis=0, keepdims=True)
        idx = jnp.min(jnp.where(work == m, eio, float(n_exp)), axis=0, keepdims=True)
        oh = eio == idx
        e_ref[k:k + 1, :] = idx.astype(I32)
        vals.append(m)
        onehots.append(oh)
        work = jnp.where(oh, -jnp.inf, work)
    exps = [jnp.exp(v - vals[0]) for v in vals]
    denom = exps[0] + exps[1] + exps[2] + exps[3]
    for k in range(TOP_K):
        gate_ref[k:k + 1, :] = exps[k] / denom
    self32 = sum(oh.astype(F32) for oh in onehots)
    before = carry[...] + jnp.dot(self32.astype(BF16), tri_ref[...], preferred_element_type=F32)
    for k in range(TOP_K):
        rank_ref[k:k + 1, :] = jnp.sum(jnp.where(onehots[k], before, 0.0), axis=0,
                                       keepdims=True).astype(I32)
    pad = jnp.zeros((8 - TOP_K, tb), I32)
    e_ref[TOP_K:8, :] = pad
    rank_ref[TOP_K:8, :] = pad
    gate_ref[TOP_K:8, :] = pad.astype(F32)
    carry[...] = carry[...] + jnp.sum(self32, axis=1, keepdims=True)
    cnt_ref[...] = jnp.broadcast_to(carry[...], cnt_ref.shape)


def _route(logits_t):
    n_exp, t = logits_t.shape
    tb = _pick(t, (512, 256, 128))
    r = lax.broadcasted_iota(I32, (tb, tb), 0)
    c = lax.broadcasted_iota(I32, (tb, tb), 1)
    tri = (r < c).astype(BF16)
    row_spec = pl.BlockSpec((8, tb), lambda i: (0, i))
    return pl.pallas_call(
        _route_kernel,
        grid=(t // tb,),
        in_specs=[pl.BlockSpec((n_exp, tb), lambda i: (0, i)),
                  pl.BlockSpec((tb, tb), lambda i: (0, 0))],
        out_specs=[row_spec, row_spec, row_spec, pl.BlockSpec((n_exp, LANES), lambda i: (0, 0))],
        out_shape=[
            jax.ShapeDtypeStruct((8, t), I32),
            jax.ShapeDtypeStruct((8, t), F32),
            jax.ShapeDtypeStruct((8, t), I32),
            jax.ShapeDtypeStruct((n_exp, LANES), F32),
        ],
        scratch_shapes=[pltpu.VMEM((n_exp, 1), F32)],
        compiler_params=_cparams(("arbitrary",)),
        name="route",
    )(logits_t, tri)


def _dispatch_kernel(pad_start_ref, pad_len_ref, tail_ref, dest_ref, hp_ref, xs_ref, zbuf, sem, zsem):
    zrows = zbuf.shape[0]

    @pl.when(pl.program_id(0) == 0)
    def _():
        zbuf[...] = jnp.zeros_like(zbuf)

        def pad_copy(r):
            return pltpu.make_async_copy(zbuf.at[pl.ds(0, 1), :], xs_ref.at[pl.ds(r, 1), :], zsem)

        def tail_copy(b):
            r = pl.multiple_of(tail_ref[0] + b * zrows, zrows)
            return pltpu.make_async_copy(zbuf, xs_ref.at[pl.ds(r, zrows), :], zsem)

        def for_pads(fn):
            def per_expert(e, carry):
                lax.fori_loop(0, pad_len_ref[e], lambda r, c: (fn(pad_start_ref[e] + r), c)[1], 0)
                return carry
            lax.fori_loop(0, pad_start_ref.shape[0], per_expert, 0)

        def for_tail(fn):
            lax.fori_loop(0, tail_ref[1], lambda b, c: (fn(b), c)[1], 0)

        for_pads(lambda r: pad_copy(r).start())
        for_tail(lambda b: tail_copy(b).start())
        for_pads(lambda r: pad_copy(0).wait())
        for_tail(lambda b: tail_copy(0).wait())

    def row_copy(grp, u, d):
        return pltpu.make_async_copy(hp_ref.at[grp, pl.ds(u, 1), :], xs_ref.at[pl.ds(d, 1), :], sem)

    def issue(grp, carry):
        for u in range(ROW_GROUP):
            for k in range(TOP_K):
                row_copy(grp, u, dest_ref[0, grp, u * TOP_K + k]).start(priority=k % 2)
        return carry

    lax.fori_loop(0, hp_ref.shape[0], issue, 0)

    def drain(grp, carry):
        for _ in range(ROW_GROUP * TOP_K):
            row_copy(0, 0, 0).wait()
        return carry

    lax.fori_loop(0, hp_ref.shape[0], drain, 0)


def _dispatch(pad_start, pad_len, tail, dest_g, hp, n_rows, zrows):
    t, half = hp.shape
    tb = dest_g.shape[1] * ROW_GROUP
    grid_spec = pltpu.PrefetchScalarGridSpec(
        num_scalar_prefetch=3,
        grid=(t // tb,),
        in_specs=[
            pl.BlockSpec((1,) + dest_g.shape[1:], lambda i, *_: (i, 0, 0), memory_space=pltpu.SMEM),
            pl.BlockSpec((tb // ROW_GROUP, ROW_GROUP, half), lambda i, *_: (i, 0, 0)),
        ],
        out_specs=pl.BlockSpec(memory_space=pl.ANY),
        scratch_shapes=[pltpu.VMEM((zrows, half), U32), pltpu.SemaphoreType.DMA(()),
                        pltpu.SemaphoreType.DMA(())],
    )
    return pl.pallas_call(
        _dispatch_kernel,
        grid_spec=grid_spec,
        out_shape=jax.ShapeDtypeStruct((n_rows, half), U32),
        compiler_params=_cparams(("arbitrary",)),
        name="dispatch",
    )(pad_start, pad_len, tail, dest_g, hp.reshape(t // ROW_GROUP, ROW_GROUP, half))


def _expert_kernel(nf, te_ref, nu_ref, xs_ref, wg_ref, wu_ref, bg_ref, bu_ref, wd_ref, bd_ref,
                   out_ref, acc, x_scr):
    del te_ref
    i = pl.program_id(0)
    j = pl.program_id(1)

    @pl.when(i < nu_ref[0])
    def _():
        @pl.when(j == 0)
        def _():
            words = xs_ref[...]
            half = words.shape[1]
            x_scr[:, :half] = lax.bitcast_convert_type(words << 16, F32).astype(BF16)
            x_scr[:, half:] = lax.bitcast_convert_type(words & jnp.uint32(0xFFFF0000), F32).astype(BF16)

        x = x_scr[...]
        gate = jnp.dot(x, wg_ref[0], preferred_element_type=F32) + bg_ref[0]
        up = jnp.dot(x, wu_ref[0], preferred_element_type=F32) + bu_ref[0]
        gate = jnp.minimum(gate, SWIGLU_LIMIT)
        up = jnp.clip(up, -SWIGLU_LIMIT, SWIGLU_LIMIT)
        act = ((up + 1.0) * gate * _sigmoid(SWIGLU_ALPHA * gate)).astype(BF16)
        part = jnp.dot(act, wd_ref[0], preferred_element_type=F32)
        if nf == 1:
            out_ref[...] = part + bd_ref[0]
        else:
            @pl.when(j == 0)
            def _():
                acc[...] = part + bd_ref[0]

            @pl.when((j > 0) & (j < nf - 1))
            def _():
                acc[...] += part

            @pl.when(j == nf - 1)
            def _():
                out_ref[...] = acc[...] + part

    @pl.when((i >= nu_ref[0]) & (j == nf - 1))
    def _():
        out_ref[...] = jnp.zeros_like(out_ref)


def _experts(tile_expert, n_used, xs, w_gate_up, b_gate_up3, w_down, b_down3, *, tm):
    n_rows, half = xs.shape
    d = 2 * half
    n_exp, _, two_f = w_gate_up.shape
    f = two_f // 2
    tf = _pick(f, (1024, 512, 256, 128))
    nf = f // tf
    n_tiles = n_rows // tm

    def tile(i, nu):
        return jnp.minimum(i, nu[0] - 1)

    grid_spec = pltpu.PrefetchScalarGridSpec(
        num_scalar_prefetch=2,
        grid=(n_tiles, nf),
        in_specs=[
            pl.BlockSpec((tm, half), lambda i, j, te, nu: (tile(i, nu), 0)),
            pl.BlockSpec((1, d, tf), lambda i, j, te, nu: (te[tile(i, nu)], 0, j)),
            pl.BlockSpec((1, d, tf), lambda i, j, te, nu: (te[tile(i, nu)], 0, nf + j)),
            pl.BlockSpec((1, 1, tf), lambda i, j, te, nu: (te[tile(i, nu)], 0, j)),
            pl.BlockSpec((1, 1, tf), lambda i, j, te, nu: (te[tile(i, nu)], 0, nf + j)),
            pl.BlockSpec((1, tf, d), lambda i, j, te, nu: (te[tile(i, nu)], j, 0)),
            pl.BlockSpec((1, 1, d), lambda i, j, te, nu: (te[tile(i, nu)], 0, 0)),
        ],
        out_specs=pl.BlockSpec((tm, d), lambda i, j, te, nu: (i, 0)),
        scratch_shapes=[pltpu.VMEM((tm, d), F32), pltpu.VMEM((tm, d), BF16)],
    )
    return pl.pallas_call(
        functools.partial(_expert_kernel, nf),
        grid_spec=grid_spec,
        out_shape=jax.ShapeDtypeStruct((n_rows, d), F32),
        compiler_params=_cparams(("arbitrary", "arbitrary")),
        name="experts",
    )(tile_expert, n_used, xs, w_gate_up, w_gate_up, b_gate_up3, b_gate_up3, w_down, b_down3)


def _combine_kernel(n, dest_ref, dest_next_ref, x1_ref, gates_ref, gf_ref, ys_ref, y_ref, gbuf, sems):
    i = pl.program_id(0)
    tb, d = x1_ref.shape
    n_grp = tb // ROW_GROUP

    def row_copy(slot, k, grp, u, r):
        return pltpu.make_async_copy(ys_ref.at[pl.ds(r, 1), :],
                                     gbuf.at[slot, k, grp, pl.ds(u, 1), :], sems.at[slot])

    def gather(d_ref, slot):
        def issue(grp, carry):
            for u in range(ROW_GROUP):
                for k in range(TOP_K):
                    row_copy(slot, k, grp, u, d_ref[0, grp, u * TOP_K + k]).start(priority=k % 2)
            return carry
        lax.fori_loop(0, n_grp, issue, 0)

    pl.when(i == 0)(lambda: gather(dest_ref, 0))
    pl.when(i + 1 < n)(lambda: gather(dest_next_ref, (i + 1) % 2))

    slot = i % 2

    def drain(grp, carry):
        for _ in range(ROW_GROUP * TOP_K):
            row_copy(slot, 0, 0, 0, 0).wait()
        return carry

    lax.fori_loop(0, n_grp, drain, 0)

    gates = gates_ref[...]
    y = x1_ref[...]
    for k in range(TOP_K):
        y = y + gates[:, k:k + 1] * gbuf[slot, k].reshape(tb, d)
    y_ref[...] = _rms(y, gf_ref[...])


def _combine(dest_g, x1, gates_tk, gf, ys, blk0, n_blk):
    d = x1.shape[1]
    tb = dest_g.shape[1] * ROW_GROUP
    dblk = (1,) + dest_g.shape[1:]
    return pl.pallas_call(
        functools.partial(_combine_kernel, n_blk),
        grid=(n_blk,),
        in_specs=[
            pl.BlockSpec(dblk, lambda i: (blk0 + i, 0, 0), memory_space=pltpu.SMEM),
            pl.BlockSpec(dblk, lambda i: (blk0 + jnp.minimum(i + 1, n_blk - 1), 0, 0),
                         memory_space=pltpu.SMEM),
            pl.BlockSpec((tb, d), lambda i: (blk0 + i, 0)),
            pl.BlockSpec((tb, TOP_K), lambda i: (blk0 + i, 0)),
            pl.BlockSpec((1, d), lambda i: (0, 0)),
            pl.BlockSpec(memory_space=pl.ANY),
        ],
        out_specs=pl.BlockSpec((tb, d), lambda i: (i, 0)),
        out_shape=jax.ShapeDtypeStruct((n_blk * tb, d), F32),
        scratch_shapes=[pltpu.VMEM((2, TOP_K, tb // ROW_GROUP, ROW_GROUP, d), F32),
                        pltpu.SemaphoreType.DMA((2,))],
        compiler_params=_cparams(("arbitrary",)),
        name="combine",
    )(dest_g, dest_g, x1, gates_tk, gf, ys)


def kernel(x_prompt, x_sample, state_gdn, cache_gdn_conv, state_hgrn, ln1_g, w_in, gdn_conv_w,
           gdn_A_log, gdn_dt_bias, gdn_norm_g, hgrn_lb_logits, hgrn_norm_g, w_out, ln2_g,
           w_router, b_router, w_gate_up, b_gate_up, w_down, b_down, ln_f_g):
    nb, seq, d = x_prompt.shape
    db, dseq, _ = x_sample.shape
    depth, _, heads, dk, dv = state_gdn.shape
    assert depth == 1 and dk == HEAD_DIM and dv == HEAD_DIM
    assert state_hgrn.shape[2:] == (heads, HEAD_DIM, HEAD_DIM)
    assert seq % CHUNK == 0 and dseq % CHUNK == 0 and 2 * heads <= LANES
    w = heads * HEAD_DIM
    assert d == 2 * w and w_in.shape[2] == 8 * w + 2 * heads
    n_exp = w_router.shape[2]
    t_p = nb * seq
    t_all = t_p + db * dseq
    n_seq = nb + db

    wi = w_in[0]
    w_main = jnp.concatenate([wi[:, :4 * w], wi[:, 4 * w + 2 * heads:]], axis=1).astype(BF16)
    w_small = jnp.pad(wi[:, 4 * w:4 * w + 2 * heads], ((0, 0), (0, LANES - 2 * heads))).astype(BF16)
    conv_w8 = jnp.pad(gdn_conv_w[0], ((0, 8 - CONV_W), (0, 0)))
    lane_pad = lambda v: jnp.pad(v, (heads, LANES - 2 * heads))
    ad = jnp.pad(jnp.stack([lane_pad(gdn_A_log[0]), lane_pad(gdn_dt_bias[0])]), ((0, 6), (0, 0)))
    lower_bounds = jnp.cumsum(jax.nn.softmax(hgrn_lb_logits.astype(F32), axis=0), axis=0)[0:1]
    zeros_state = jnp.zeros((nb, heads, HEAD_DIM, HEAD_DIM), F32)
    sg0 = jnp.concatenate([zeros_state, state_gdn[0]], axis=0)
    sh0 = jnp.concatenate([zeros_state, state_hgrn[0]], axis=0)
    cv0 = jnp.pad(jnp.concatenate([jnp.zeros((nb, CONV_W - 1, 3 * w), F32), cache_gdn_conv[0]], axis=0),
                  ((0, 0), (8 - (CONV_W - 1), 0), (0, 0)))

    xp = x_prompt.reshape(t_p, d)
    xs_tok = x_sample.reshape(db * dseq, d)

    mixed, sg, sh, cv = _front(xp, xs_tok, ln1_g, w_main, w_small, conv_w8, ad, gdn_norm_g,
                               lower_bounds, hgrn_norm_g, sg0, sh0, cv0, heads=heads,
                               n_prompt_seq=nb, n_prompt_chunks=seq // CHUNK,
                               n_sample_chunks=dseq // CHUNK)
    x1, hp, logits_t = _outproj(xp, xs_tok, mixed, w_out[0].astype(BF16), ln2_g,
                                w_router[0].T, b_router[0][:, None])

    e_idx, gates, ranks, counts = _route(logits_t)
    tm_e = 512 if t_all * TOP_K >= 512 * n_exp else 128
    cnt = counts[:, 0].astype(I32)
    padded = (cnt + tm_e - 1) // tm_e * tm_e
    seg_end = jnp.cumsum(padded)
    seg_start = seg_end - padded
    n_tiles = (t_all * TOP_K + n_exp * (tm_e - 1)) // tm_e
    onehot = e_idx[:TOP_K, :, None] == jnp.arange(n_exp, dtype=I32)
    dest = jnp.sum(jnp.where(onehot, seg_start, 0), axis=-1) + ranks[:TOP_K]
    tb = _pick(math.gcd(t_p, t_all - t_p), (256, 128))
    dest_g = dest.reshape(TOP_K, t_all // tb, tb // ROW_GROUP, ROW_GROUP).transpose(1, 2, 3, 0)
    dest_g = dest_g.reshape(t_all // tb, tb // ROW_GROUP, ROW_GROUP * TOP_K)
    tile_start = jnp.arange(n_tiles, dtype=I32) * tm_e
    tile_expert = jnp.minimum(jnp.sum(seg_end[None, :] <= tile_start[:, None], axis=1),
                              n_exp - 1).astype(I32)
    n_used = (seg_end[-1:] // tm_e).astype(I32)

    zrows = 64
    tail = jnp.stack([seg_end[-1], (n_tiles * tm_e - seg_end[-1]) // zrows]).astype(I32)
    xs = _dispatch(seg_start + cnt, padded - cnt, tail, dest_g, hp, n_tiles * tm_e, zrows)
    ys = _experts(tile_expert, n_used, xs, w_gate_up[0].astype(BF16), b_gate_up[0][:, None, :],
                  w_down[0].astype(BF16), b_down[0][:, None, :], tm=tm_e)
    gates_tk = gates[:TOP_K].T
    gf = ln_f_g[None, :]
    y_prompt = _combine(dest_g, x1, gates_tk, gf, ys, 0, t_p // tb).reshape(nb, seq, d)
    y_sample = _combine(dest_g, x1, gates_tk, gf, ys, t_p // tb, (t_all - t_p) // tb).reshape(db, dseq, d)
    cache = cv[:, 8 - (CONV_W - 1):, :]
    return (y_prompt, y_sample,
            sg[None, :nb], cache[None, :nb], sh[None, :nb],
            sg[None, nb:], cache[None, nb:], sh[None, nb:])
```

```python
import functools
import math

import jax
import jax.numpy as jnp
from jax import lax
from jax.experimental import pallas as pl
from jax.experimental.pallas import tpu as pltpu

F32 = jnp.float32
BF16 = jnp.bfloat16
U32 = jnp.uint32
I32 = jnp.int32

CHUNK = 64
CONV_W = 4
TOP_K = 4
NORM_EPS = 1e-6
L2_EPS = 1e-6
SWIGLU_LIMIT = 7.0
SWIGLU_ALPHA = 1.702
HEAD_DIM = 128
LANES = 128
HGRN_SUB = 16
ROW_GROUP = 8
VMEM_LIMIT = 56 * 1024 * 1024


def _pick(n, cands):
    for c in cands:
        if n % c == 0:
            return c
    raise ValueError(f"no tile in {cands} divides {n}")


def _cparams(sem):
    return pltpu.CompilerParams(dimension_semantics=sem, vmem_limit_bytes=VMEM_LIMIT)


def _sigmoid(x):
    return 1.0 / (1.0 + jnp.exp(-x))


def _silu(x):
    return x * _sigmoid(x)


def _softplus(x):
    return jnp.maximum(x, 0.0) + jnp.log1p(jnp.exp(-jnp.abs(x)))


def _rms(x, g):
    return x * lax.rsqrt(jnp.mean(x * x, axis=-1, keepdims=True) + NORM_EPS) * g


def _dot(a, b):
    return jnp.dot(a.astype(BF16), b.astype(BF16), preferred_element_type=F32)


def _dot_nt(a, b):
    return lax.dot_general(a.astype(BF16), b.astype(BF16), (((1,), (1,)), ((), ())),
                           preferred_element_type=F32)


def _dot_tn(a, b):
    return lax.dot_general(a.astype(BF16), b.astype(BF16), (((0,), (0,)), ((), ())),
                           preferred_element_type=F32)


GROUP = 4
PROJ_PIECES = 4


def _two_stream_specs(tm, d, n_prompt_blocks, block_of):
    return [
        pl.BlockSpec((tm, d), lambda s: (jnp.minimum(block_of(s), n_prompt_blocks - 1), 0)),
        pl.BlockSpec((tm, d), lambda s: (jnp.maximum(block_of(s) - n_prompt_blocks, 0), 0)),
    ]


def _cumsum_rows(x, row_idx):
    s = 1
    while s < x.shape[0]:
        x = x + jnp.where(row_idx >= s, pltpu.roll(x, s, 0), 0.0)
        s *= 2
    return x


def _front_kernel(heads, n_prompt_chunks, n_prompt_total, n_sample_chunks, n_prompt_groups,
                  xp_ref, xs_ref, g1_ref, wm_ref, ws_ref,
                  cw_ref, ad_ref, gng_ref, lb_ref, hng_ref, sg0_ref, sh0_ref, cv0_ref,
                  mix_ref, sg_ref, sh_ref, cv_ref,
                  h_scr, pm_a, ps_a, pm_b, ps_b, cbuf, qkv_scr):
    s = pl.program_id(0)
    g = s // GROUP

    @pl.when(s == 0)
    def _():
        for buf in (pm_a, ps_a, pm_b, ps_b):
            buf[...] = jnp.zeros_like(buf)

    args = (heads, n_prompt_chunks, n_prompt_total, n_sample_chunks, n_prompt_groups,
            xp_ref, xs_ref, g1_ref, wm_ref, ws_ref,
            cw_ref, ad_ref, gng_ref, lb_ref, hng_ref, sg0_ref, sh0_ref, cv0_ref,
            mix_ref, sg_ref, sh_ref, cv_ref, h_scr, cbuf, qkv_scr)
    pl.when(g % 2 == 0)(lambda: _front_step(*args, pm_a, ps_a, pm_b, ps_b))
    pl.when(g % 2 == 1)(lambda: _front_step(*args, pm_b, ps_b, pm_a, ps_a))


def _front_step(heads, n_prompt_chunks, n_prompt_total, n_sample_chunks, n_prompt_groups,
                xp_ref, xs_ref, g1_ref, wm_ref, ws_ref,
                cw_ref, ad_ref, gng_ref, lb_ref, hng_ref, sg0_ref, sh0_ref, cv0_ref,
                mix_ref, sg_ref, sh_ref, cv_ref, h_scr, cbuf, qkv_scr,
                pm_w, ps_w, pm_r, ps_r):
    s = pl.program_id(0)
    w = heads * HEAD_DIM
    L = CHUNK
    tw = wm_ref.shape[1]
    pw = tw // PROJ_PIECES
    g = s // GROUP
    j = s % GROUP
    c = jnp.maximum(s - GROUP, 0)
    r0 = pl.multiple_of((c % GROUP) * L, L)
    is_start = jnp.where(c < n_prompt_total, (c % n_prompt_chunks) == 0,
                         ((c - n_prompt_total) % n_sample_chunks) == 0)

    def norm_from(x_ref):
        hb = _rms(x_ref[...], g1_ref[...]).astype(BF16)
        h_scr[...] = hb
        ps_w[...] = jnp.dot(hb, ws_ref[...], preferred_element_type=F32)

    pl.when((j == 0) & (g < n_prompt_groups))(lambda: norm_from(xp_ref))
    pl.when((j == 0) & (g >= n_prompt_groups))(lambda: norm_from(xs_ref))

    def proj_piece(p):
        cols = slice(p * pw, (p + 1) * pw)
        pm_w[j, :, cols] = jnp.dot(h_scr[...], wm_ref[:, cols], preferred_element_type=F32)

    def pm(col0, width=HEAD_DIM):
        tile, off = divmod(col0, tw)
        assert off + width <= tw
        return pm_r[tile, pl.ds(r0, L), off:off + width]

    @pl.when(is_start)
    def _():
        sg_ref[...] = sg0_ref[...]
        sh_ref[...] = sh0_ref[...]
        cbuf[0:8, :] = cv0_ref[0]

    proj_piece(0)
    proj_piece(1)
    proj_piece(2)

    for c0 in range(0, 3 * w, tw):
        c1 = min(c0 + tw, 3 * w)
        cbuf[8:8 + L, c0:c1] = pm(c0, c1 - c0)
    conv = cbuf[5:5 + L, :] * cw_ref[0:1, :]
    for tap in range(1, CONV_W):
        conv = conv + cbuf[5 + tap:5 + tap + L, :] * cw_ref[tap:tap + 1, :]
    tail = cbuf[L:L + 8, :]
    cv_ref[0] = tail
    cbuf[0:8, :] = tail
    qkv_scr[...] = _silu(conv)

    row = lax.broadcasted_iota(I32, (L, L), 0)
    col = lax.broadcasted_iota(I32, (L, L), 1)
    incl = row >= col
    strict = row > col
    eye = (row == col).astype(F32)
    lvl_masks = []
    for lg2 in range(L.bit_length() - 1):
        same_pair = (row >> (lg2 + 1)) == (col >> (lg2 + 1))
        lower_left = (((row >> lg2) & 1) == 1) & (((col >> lg2) & 1) == 0)
        lvl_masks.append(same_pair & lower_left)
    row_l = lax.broadcasted_iota(I32, (L, HEAD_DIM), 0)
    sub_row = lax.broadcasted_iota(I32, (HGRN_SUB, L), 0)
    sub_col = lax.broadcasted_iota(I32, (HGRN_SUB, L), 1)
    hs = range(heads)
    lane = lambda base, h: slice(base + h * HEAD_DIM, base + (h + 1) * HEAD_DIM)

    def gated_norm(o, g_ref, z):
        return o * lax.rsqrt(jnp.mean(o * o, axis=-1, keepdims=True) + NORM_EPS) * g_ref[...] * _silu(z)

    def hgrn_front(h):
        qh = _silu(pm(4 * w + h * HEAD_DIM))
        lbh = lb_ref[:, lane(0, h)]
        f = lbh + (1.0 - lbh) * _sigmoid(pm(5 * w + h * HEAD_DIM))
        kh = 1.0 - f
        v = pm(6 * w + h * HEAD_DIM)
        bc = _cumsum_rows(jnp.log(f), row_l)
        a_parts = []
        for i in range(L // HGRN_SUB):
            r0 = i * HGRN_SUB
            r1 = r0 + HGRN_SUB
            e_i = bc[r0 - 1:r0, :] if i > 0 else jnp.zeros((1, HEAD_DIM), F32)
            qt = qh[r0:r1] * jnp.exp(bc[r0:r1] - e_i)
            kt = kh * jnp.exp(jnp.where(row_l < r1, e_i - bc, 0.0))
            a_parts.append(jnp.where(sub_col <= sub_row + r0, _dot_nt(qt, kt), 0.0))
        s = sh_ref[0, h]
        o_inter = _dot(qh * jnp.exp(bc), s)
        ft_col = jnp.exp(bc.T[:, L - 1:L])
        sh_ref[0, h] = ft_col * s + _dot_tn(kh * jnp.exp(bc[L - 1:L, :] - bc), v)
        return jnp.concatenate(a_parts, axis=0), v, o_inter

    def hgrn_back(h, front):
        attn_h, v, o_inter = front
        o = gated_norm(o_inter + _dot(attn_h, v), hng_ref, pm(7 * w + h * HEAD_DIM))
        mix_ref[:, lane(w, h)] = o.astype(mix_ref.dtype)

    fronts = {}

    def hgrn_slot(i):
        if i < heads:
            fronts[i] = hgrn_front(i)
        if 0 <= i - 1 < heads:
            hgrn_back(i - 1, fronts.pop(i - 1))

    ps = ps_r[pl.ds(r0, L), :]
    beta_all = _sigmoid(ps)
    g_all = -jnp.exp(ad_ref[0:1, :]) * _softplus(ps + ad_ref[1:2, :])
    gc_all = _cumsum_rows(g_all, row_l)
    gc_t = gc_all.T
    q, k, v, bcol, gcol, egc, glast, dec_incl = ([] for _ in range(8))
    for h in hs:
        q_c = qkv_scr[:, lane(0, h)]
        k_c = qkv_scr[:, lane(w, h)]
        v.append(qkv_scr[:, lane(2 * w, h)])
        q.append(q_c * lax.rsqrt(jnp.sum(q_c * q_c, axis=-1, keepdims=True) + L2_EPS) * (HEAD_DIM ** -0.5))
        k.append(k_c * lax.rsqrt(jnp.sum(k_c * k_c, axis=-1, keepdims=True) + L2_EPS))
        bcol.append(beta_all[:, h:h + 1])
        gcol.append(gc_all[:, heads + h:heads + h + 1])
        glast.append(gc_all[L - 1:L, heads + h:heads + h + 1])
        egc.append(jnp.exp(gcol[h]))
        grow = gc_t[heads + h:heads + h + 1, :]
        dec_incl.append(jnp.where(incl, jnp.exp(jnp.where(incl, gcol[h] - grow, 0.0)), 0.0))
    kk = [_dot_nt(k[h], k[h]) for h in hs]
    qk = [_dot_nt(q[h], k[h]) for h in hs]
    a_mat = [bcol[h] * kk[h] * jnp.where(strict, dec_incl[h], 0.0) for h in hs]
    attn = [qk[h] * dec_incl[h] for h in hs]
    s_old = [sg_ref[0, h] for h in hs]
    o_inter = [_dot(q[h] * egc[h], s_old[h]) for h in hs]
    x = [eye - jnp.where(lvl_masks[0], a_mat[h], 0.0) for h in hs]
    piece_after_level = {1: 3}
    assert PROJ_PIECES == 4 and len(lvl_masks) > 4
    slot = 0
    for lvl, m in enumerate(lvl_masks[1:]):
        y = [_dot(x[h], jnp.where(m, a_mat[h], 0.0)) for h in hs]
        hgrn_slot(slot)
        x = [x[h] - _dot(y[h], x[h]) for h in hs]
        hgrn_slot(slot + 1)
        slot += 2
        if lvl in piece_after_level:
            proj_piece(piece_after_level[lvl])
    sol = [_dot(x[h], jnp.concatenate([bcol[h] * v[h], (bcol[h] * egc[h]) * k[h]], axis=-1)) for h in hs]
    while slot <= heads:
        hgrn_slot(slot)
        slot += 1
    delta = [sol[h][:, :HEAD_DIM] - _dot(sol[h][:, HEAD_DIM:], s_old[h]) for h in hs]
    o_intra = [_dot(attn[h], delta[h]) for h in hs]
    s_add = [_dot_tn(k[h] * jnp.exp(glast[h] - gcol[h]), delta[h]) for h in hs]
    for h in hs:
        sg_ref[0, h] = jnp.exp(glast[h]) * s_old[h] + s_add[h]
        o = gated_norm(o_inter[h] + o_intra[h], gng_ref, pm(3 * w + h * HEAD_DIM))
        mix_ref[:, lane(0, h)] = o.astype(mix_ref.dtype)


def _front(xp, xs, g1, w_main, w_small, conv_w, ad, gng, lb, hng, sg0, sh0, cv0, *, heads,
           n_prompt_seq, n_prompt_chunks, n_sample_chunks):
    d = xp.shape[1]
    t = xp.shape[0] + xs.shape[0]
    nm = w_main.shape[1]
    w = heads * HEAD_DIM
    n_seq = sg0.shape[0]
    n_chunks = t // CHUNK
    n_prompt_total = n_prompt_seq * n_prompt_chunks
    rows_g = GROUP * CHUNK
    assert xp.shape[0] % rows_g == 0 and xs.shape[0] % rows_g == 0 and nm % (GROUP * PROJ_PIECES * LANES) == 0
    n_prompt_groups = xp.shape[0] // rows_g
    n_groups = t // rows_g
    tw = nm // GROUP

    def seq_of(s):
        c = jnp.maximum(s - GROUP, 0)
        return jnp.where(c < n_prompt_total, c // n_prompt_chunks,
                         n_prompt_seq + (c - n_prompt_total) // n_sample_chunks)

    const = lambda shape: pl.BlockSpec(shape, lambda s: (0,) * len(shape))
    st_spec = pl.BlockSpec((1, heads, HEAD_DIM, HEAD_DIM), lambda s: (seq_of(s), 0, 0, 0))
    cv_spec = pl.BlockSpec((1, 8, 3 * w), lambda s: (seq_of(s), 0, 0))
    kern = functools.partial(_front_kernel, heads, n_prompt_chunks, n_prompt_total, n_sample_chunks,
                             n_prompt_groups)
    return pl.pallas_call(
        kern,
        grid=(n_chunks + GROUP,),
        in_specs=_two_stream_specs(rows_g, d, n_prompt_groups,
                                   lambda s: jnp.minimum(s // GROUP, n_groups - 1)) + [
            const((1, d)),
            pl.BlockSpec((d, tw), lambda s: (0, s % GROUP)),
            const((d, LANES)),
            const((8, 3 * w)), const((8, LANES)), const((1, HEAD_DIM)), const((1, w)),
            const((1, HEAD_DIM)),
            st_spec, st_spec, cv_spec,
        ],
        out_specs=[
            pl.BlockSpec((CHUNK, 2 * w), lambda s: (jnp.maximum(s - GROUP, 0), 0)),
            st_spec, st_spec, cv_spec,
        ],
        out_shape=[
            jax.ShapeDtypeStruct((t, 2 * w), BF16),
            jax.ShapeDtypeStruct((n_seq, heads, HEAD_DIM, HEAD_DIM), F32),
            jax.ShapeDtypeStruct((n_seq, heads, HEAD_DIM, HEAD_DIM), F32),
            jax.ShapeDtypeStruct((n_seq, 8, 3 * w), F32),
        ],
        scratch_shapes=[
            pltpu.VMEM((rows_g, d), BF16),
            pltpu.VMEM((GROUP, rows_g, tw), F32), pltpu.VMEM((rows_g, LANES), F32),
            pltpu.VMEM((GROUP, rows_g, tw), F32), pltpu.VMEM((rows_g, LANES), F32),
            pltpu.VMEM((CHUNK + 8, 3 * w), F32), pltpu.VMEM((CHUNK, 3 * w), F32),
        ],
        compiler_params=_cparams(("arbitrary",)),
        name="front",
    )(xp, xs, g1, w_main, w_small, conv_w, ad, gng, lb, hng, sg0, sh0, cv0)


def _outproj_kernel(n_prompt_blocks, xp_ref, xs_ref, m_ref, wo_ref, g_ref, wr_ref, br_ref,
                    x1_ref, hp_ref, lg_ref):
    def body(x_ref):
        x1 = x_ref[...] + jnp.dot(m_ref[...], wo_ref[...], preferred_element_type=F32)
        x1_ref[...] = x1
        h2 = _rms(x1, g_ref[...])
        half = h2.shape[1] // 2
        bits = lax.bitcast_convert_type(h2.astype(BF16).astype(F32), U32)
        hp_ref[...] = (bits[:, :half] >> 16) | (bits[:, half:] & jnp.uint32(0xFFFF0000))
        wr = wr_ref[...]
        wr_hi = wr.astype(BF16)
        wr_lo = (wr - wr_hi.astype(F32)).astype(BF16)
        h_hi = h2.astype(BF16)
        h_lo = (h2 - h_hi.astype(F32)).astype(BF16)
        nt = lambda a, b: lax.dot_general(a, b, (((1,), (1,)), ((), ())), preferred_element_type=F32)
        lg_ref[...] = nt(wr_hi, h_hi) + nt(wr_hi, h_lo) + nt(wr_lo, h_hi) + br_ref[...]

    i = pl.program_id(0)
    pl.when(i < n_prompt_blocks)(lambda: body(xp_ref))
    pl.when(i >= n_prompt_blocks)(lambda: body(xs_ref))


def _outproj(xp, xs, mixed, w_out, g2, w_router_t, b_router_col):
    d = xp.shape[1]
    t = xp.shape[0] + xs.shape[0]
    e = w_router_t.shape[0]
    tm = _pick(math.gcd(xp.shape[0], xs.shape[0]), (512, 256, 128))
    n_prompt_blocks = xp.shape[0] // tm
    return pl.pallas_call(
        functools.partial(_outproj_kernel, n_prompt_blocks),
        grid=(t // tm,),
        in_specs=_two_stream_specs(tm, d, n_prompt_blocks, lambda i: i) + [
            pl.BlockSpec((tm, d), lambda i: (i, 0)),
            pl.BlockSpec((d, d), lambda i: (0, 0), pipeline_mode=pl.Buffered(1)),
            pl.BlockSpec((1, d), lambda i: (0, 0)),
            pl.BlockSpec((e, d), lambda i: (0, 0)),
            pl.BlockSpec((e, 1), lambda i: (0, 0)),
        ],
        out_specs=[
            pl.BlockSpec((tm, d), lambda i: (i, 0)),
            pl.BlockSpec((tm, d // 2), lambda i: (i, 0)),
            pl.BlockSpec((e, tm), lambda i: (0, i)),
        ],
        out_shape=[
            jax.ShapeDtypeStruct((t, d), F32),
            jax.ShapeDtypeStruct((t, d // 2), U32),
            jax.ShapeDtypeStruct((e, t), F32),
        ],
        compiler_params=_cparams(("arbitrary",)),
        name="outproj",
    )(xp, xs, mixed, w_out, g2, w_router_t, b_router_col)


def _route_kernel(lg_ref, tri_ref, e_ref, gate_ref, rank_ref, cnt_ref, carry):
    @pl.when(pl.program_id(0) == 0)
    def _():
        carry[...] = jnp.zeros_like(carry)

    lg = lg_ref[...]
    n_exp, tb = lg.shape
    eio = lax.broadcasted_iota(I32, (n_exp, tb), 0).astype(F32)
    work = lg
    vals, onehots = [], []
    for k in range(TOP_K):
        m = jnp.max(work, axis=0, keepdims=True)
        idx = jnp.min(jnp.where(work == m, eio, float(n_exp)), axis=0, keepdims=True)
        oh = eio == idx
        e_ref[k:k + 1, :] = idx.astype(I32)
        vals.append(m)
        onehots.append(oh)
        work = jnp.where(oh, -jnp.inf, work)
    exps = [jnp.exp(v - vals[0]) for v in vals]
    denom = exps[0] + exps[1] + exps[2] + exps[3]
    for k in range(TOP_K):
        gate_ref[k:k + 1, :] = exps[k] / denom
    self32 = sum(oh.astype(F32) for oh in onehots)
    before = carry[...] + jnp.dot(self32.astype(BF16), tri_ref[...], preferred_element_type=F32)
    for k in range(TOP_K):
        rank_ref[k:k + 1, :] = jnp.sum(jnp.where(onehots[k], before, 0.0), axis=0,
                                       keepdims=True).astype(I32)
    pad = jnp.zeros((8 - TOP_K, tb), I32)
    e_ref[TOP_K:8, :] = pad
    rank_ref[TOP_K:8, :] = pad
    gate_ref[TOP_K:8, :] = pad.astype(F32)
    carry[...] = carry[...] + jnp.sum(self32, axis=1, keepdims=True)
    cnt_ref[...] = jnp.broadcast_to(carry[...], cnt_ref.shape)


def _route(logits_t):
    n_exp, t = logits_t.shape
    tb = _pick(t, (512, 256, 128))
    r = lax.broadcasted_iota(I32, (tb, tb), 0)
    c = lax.broadcasted_iota(I32, (tb, tb), 1)
    tri = (r < c).astype(BF16)
    row_spec = pl.BlockSpec((8, tb), lambda i: (0, i))
    return pl.pallas_call(
        _route_kernel,
        grid=(t // tb,),
        in_specs=[pl.BlockSpec((n_exp, tb), lambda i: (0, i)),
                  pl.BlockSpec((tb, tb), lambda i: (0, 0))],
        out_specs=[row_spec, row_spec, row_spec, pl.BlockSpec((n_exp, LANES), lambda i: (0, 0))],
        out_shape=[
            jax.ShapeDtypeStruct((8, t), I32),
            jax.ShapeDtypeStruct((8, t), F32),
            jax.ShapeDtypeStruct((8, t), I32),
            jax.ShapeDtypeStruct((n_exp, LANES), F32),
        ],
        scratch_shapes=[pltpu.VMEM((n_exp, 1), F32)],
        compiler_params=_cparams(("arbitrary",)),
        name="route",
    )(logits_t, tri)


def _dispatch_kernel(pad_start_ref, pad_len_ref, tail_ref, dest_ref, hp_ref, xs_ref, zbuf, sem, zsem):
    zrows = zbuf.shape[0]

    @pl.when(pl.program_id(0) == 0)
    def _():
        zbuf[...] = jnp.zeros_like(zbuf)

        def pad_copy(r):
            return pltpu.make_async_copy(zbuf.at[pl.ds(0, 1), :], xs_ref.at[pl.ds(r, 1), :], zsem)

        def tail_copy(b):
            r = pl.multiple_of(tail_ref[0] + b * zrows, zrows)
            return pltpu.make_async_copy(zbuf, xs_ref.at[pl.ds(r, zrows), :], zsem)

        def for_pads(fn):
            def per_expert(e, carry):
                lax.fori_loop(0, pad_len_ref[e], lambda r, c: (fn(pad_start_ref[e] + r), c)[1], 0)
                return carry
            lax.fori_loop(0, pad_start_ref.shape[0], per_expert, 0)

        def for_tail(fn):
            lax.fori_loop(0, tail_ref[1], lambda b, c: (fn(b), c)[1], 0)

        for_pads(lambda r: pad_copy(r).start())
        for_tail(lambda b: tail_copy(b).start())
        for_pads(lambda r: pad_copy(0).wait())
        for_tail(lambda b: tail_copy(0).wait())

    def row_copy(grp, u, d):
        return pltpu.make_async_copy(hp_ref.at[grp, pl.ds(u, 1), :], xs_ref.at[pl.ds(d, 1), :], sem)

    def issue(grp, carry):
        for u in range(ROW_GROUP):
            for k in range(TOP_K):
                row_copy(grp, u, dest_ref[0, grp, u * TOP_K + k]).start(priority=k % 2)
        return carry

    lax.fori_loop(0, hp_ref.shape[0], issue, 0)

    def drain(grp, carry):
        for _ in range(ROW_GROUP * TOP_K):
            row_copy(0, 0, 0).wait()
        return carry

    lax.fori_loop(0, hp_ref.shape[0], drain, 0)


def _dispatch(pad_start, pad_len, tail, dest_g, hp, n_rows, zrows):
    t, half = hp.shape
    tb = dest_g.shape[1] * ROW_GROUP
    grid_spec = pltpu.PrefetchScalarGridSpec(
        num_scalar_prefetch=3,
        grid=(t // tb,),
        in_specs=[
            pl.BlockSpec((1,) + dest_g.shape[1:], lambda i, *_: (i, 0, 0), memory_space=pltpu.SMEM),
            pl.BlockSpec((tb // ROW_GROUP, ROW_GROUP, half), lambda i, *_: (i, 0, 0)),
        ],
        out_specs=pl.BlockSpec(memory_space=pl.ANY),
        scratch_shapes=[pltpu.VMEM((zrows, half), U32), pltpu.SemaphoreType.DMA(()),
                        pltpu.SemaphoreType.DMA(())],
    )
    return pl.pallas_call(
        _dispatch_kernel,
        grid_spec=grid_spec,
        out_shape=jax.ShapeDtypeStruct((n_rows, half), U32),
        compiler_params=_cparams(("arbitrary",)),
        name="dispatch",
    )(pad_start, pad_len, tail, dest_g, hp.reshape(t // ROW_GROUP, ROW_GROUP, half))


def _expert_kernel(nf, te_ref, nu_ref, xs_ref, wg_ref, wu_ref, bg_ref, bu_ref, wd_ref, bd_ref,
                   out_ref, acc, x_scr):
    del te_ref
    i = pl.program_id(0)
    j = pl.program_id(1)

    @pl.when(i < nu_ref[0])
    def _():
        @pl.when(j == 0)
        def _():
            words = xs_ref[...]
            half = words.shape[1]
            x_scr[:, :half] = lax.bitcast_convert_type(words << 16, F32).astype(BF16)
            x_scr[:, half:] = lax.bitcast_convert_type(words & jnp.uint32(0xFFFF0000), F32).astype(BF16)

        x = x_scr[...]
        gate = jnp.dot(x, wg_ref[0], preferred_element_type=F32) + bg_ref[0]
        up = jnp.dot(x, wu_ref[0], preferred_element_type=F32) + bu_ref[0]
        gate = jnp.minimum(gate, SWIGLU_LIMIT)
        up = jnp.clip(up, -SWIGLU_LIMIT, SWIGLU_LIMIT)
        act = ((up + 1.0) * gate * _sigmoid(SWIGLU_ALPHA * gate)).astype(BF16)
        part = jnp.dot(act, wd_ref[0], preferred_element_type=F32)
        if nf == 1:
            out_ref[...] = part + bd_ref[0]
        else:
            @pl.when(j == 0)
            def _():
                acc[...] = part + bd_ref[0]

            @pl.when((j > 0) & (j < nf - 1))
            def _():
                acc[...] += part

            @pl.when(j == nf - 1)
            def _():
                out_ref[...] = acc[...] + part

    @pl.when((i >= nu_ref[0]) & (j == nf - 1))
    def _():
        out_ref[...] = jnp.zeros_like(out_ref)


def _experts(tile_expert, n_used, xs, w_gate_up, b_gate_up3, w_down, b_down3, *, tm):
    n_rows, half = xs.shape
    d = 2 * half
    n_exp, _, two_f = w_gate_up.shape
    f = two_f // 2
    n_tiles = n_rows // tm
    w_bytes = 3 * d * f * 2
    tile_bytes = tm * (2 * half * 4 + 2 * d * 4 + d * 2 + 2 * f * 4)
    resident = w_bytes + tile_bytes <= (VMEM_LIMIT * 7) // 8
    tf = f if resident else _pick(f, (1024, 512, 256, 128))
    nf = f // tf
    w_mode = dict(pipeline_mode=pl.Buffered(1)) if resident else {}

    def tile(i, nu):
        return jnp.minimum(i, nu[0] - 1)

    grid_spec = pltpu.PrefetchScalarGridSpec(
        num_scalar_prefetch=2,
        grid=(n_tiles, nf),
        in_specs=[
            pl.BlockSpec((tm, half), lambda i, j, te, nu: (tile(i, nu), 0)),
            pl.BlockSpec((1, d, tf), lambda i, j, te, nu: (te[tile(i, nu)], 0, j), **w_mode),
            pl.BlockSpec((1, d, tf), lambda i, j, te, nu: (te[tile(i, nu)], 0, nf + j), **w_mode),
            pl.BlockSpec((1, 1, tf), lambda i, j, te, nu: (te[tile(i, nu)], 0, j)),
            pl.BlockSpec((1, 1, tf), lambda i, j, te, nu: (te[tile(i, nu)], 0, nf + j)),
            pl.BlockSpec((1, tf, d), lambda i, j, te, nu: (te[tile(i, nu)], j, 0), **w_mode),
            pl.BlockSpec((1, 1, d), lambda i, j, te, nu: (te[tile(i, nu)], 0, 0)),
        ],
        out_specs=pl.BlockSpec((tm, d), lambda i, j, te, nu: (i, 0)),
        scratch_shapes=[pltpu.VMEM((tm, d) if nf > 1 else (8, LANES), F32),
                        pltpu.VMEM((tm, d), BF16)],
    )
    return pl.pallas_call(
        functools.partial(_expert_kernel, nf),
        grid_spec=grid_spec,
        out_shape=jax.ShapeDtypeStruct((n_rows, d), F32),
        compiler_params=_cparams(("arbitrary", "arbitrary")),
        name="experts",
    )(tile_expert, n_used, xs, w_gate_up, w_gate_up, b_gate_up3, b_gate_up3, w_down, b_down3)


def _combine_kernel(n, dest_ref, dest_next_ref, x1_ref, gates_ref, gf_ref, ys_ref, y_ref, gbuf, sems):
    i = pl.program_id(0)
    tb, d = x1_ref.shape
    n_grp = tb // ROW_GROUP

    def row_copy(slot, k, grp, u, r):
        return pltpu.make_async_copy(ys_ref.at[pl.ds(r, 1), :],
                                     gbuf.at[slot, k, grp, pl.ds(u, 1), :], sems.at[slot])

    def gather(d_ref, slot):
        def issue(grp, carry):
            for u in range(ROW_GROUP):
                for k in range(TOP_K):
                    row_copy(slot, k, grp, u, d_ref[0, grp, u * TOP_K + k]).start(priority=k % 2)
            return carry
        lax.fori_loop(0, n_grp, issue, 0)

    pl.when(i == 0)(lambda: gather(dest_ref, 0))
    pl.when(i + 1 < n)(lambda: gather(dest_next_ref, (i + 1) % 2))

    slot = i % 2

    def drain(grp, carry):
        for _ in range(ROW_GROUP * TOP_K):
            row_copy(slot, 0, 0, 0, 0).wait()
        return carry

    lax.fori_loop(0, n_grp, drain, 0)

    gates = gates_ref[...]
    y = x1_ref[...]
    for k in range(TOP_K):
        y = y + gates[:, k:k + 1] * gbuf[slot, k].reshape(tb, d)
    y_ref[...] = _rms(y, gf_ref[...])


def _combine(dest_g, x1, gates_tk, gf, ys, blk0, n_blk):
    d = x1.shape[1]
    tb = dest_g.shape[1] * ROW_GROUP
    dblk = (1,) + dest_g.shape[1:]
    return pl.pallas_call(
        functools.partial(_combine_kernel, n_blk),
        grid=(n_blk,),
        in_specs=[
            pl.BlockSpec(dblk, lambda i: (blk0 + i, 0, 0), memory_space=pltpu.SMEM),
            pl.BlockSpec(dblk, lambda i: (blk0 + jnp.minimum(i + 1, n_blk - 1), 0, 0),
                         memory_space=pltpu.SMEM),
            pl.BlockSpec((tb, d), lambda i: (blk0 + i, 0)),
            pl.BlockSpec((tb, TOP_K), lambda i: (blk0 + i, 0)),
            pl.BlockSpec((1, d), lambda i: (0, 0)),
            pl.BlockSpec(memory_space=pl.ANY),
        ],
        out_specs=pl.BlockSpec((tb, d), lambda i: (i, 0)),
        out_shape=jax.ShapeDtypeStruct((n_blk * tb, d), F32),
        scratch_shapes=[pltpu.VMEM((2, TOP_K, tb // ROW_GROUP, ROW_GROUP, d), F32),
                        pltpu.SemaphoreType.DMA((2,))],
        compiler_params=_cparams(("arbitrary",)),
        name="combine",
    )(dest_g, dest_g, x1, gates_tk, gf, ys)


def kernel(x_prompt, x_sample, state_gdn, cache_gdn_conv, state_hgrn, ln1_g, w_in, gdn_conv_w,
           gdn_A_log, gdn_dt_bias, gdn_norm_g, hgrn_lb_logits, hgrn_norm_g, w_out, ln2_g,
           w_router, b_router, w_gate_up, b_gate_up, w_down, b_down, ln_f_g):
    nb, seq, d = x_prompt.shape
    db, dseq, _ = x_sample.shape
    depth, _, heads, dk, dv = state_gdn.shape
    assert depth == 1 and dk == HEAD_DIM and dv == HEAD_DIM
    assert state_hgrn.shape[2:] == (heads, HEAD_DIM, HEAD_DIM)
    assert seq % CHUNK == 0 and dseq % CHUNK == 0 and 2 * heads <= LANES
    w = heads * HEAD_DIM
    assert d == 2 * w and w_in.shape[2] == 8 * w + 2 * heads
    n_exp = w_router.shape[2]
    t_p = nb * seq
    t_all = t_p + db * dseq
    n_seq = nb + db

    wi = w_in[0]
    w_main = jnp.concatenate([wi[:, :4 * w], wi[:, 4 * w + 2 * heads:]], axis=1).astype(BF16)
    w_small = jnp.pad(wi[:, 4 * w:4 * w + 2 * heads], ((0, 0), (0, LANES - 2 * heads))).astype(BF16)
    conv_w8 = jnp.pad(gdn_conv_w[0], ((0, 8 - CONV_W), (0, 0)))
    lane_pad = lambda v: jnp.pad(v, (heads, LANES - 2 * heads))
    ad = jnp.pad(jnp.stack([lane_pad(gdn_A_log[0]), lane_pad(gdn_dt_bias[0])]), ((0, 6), (0, 0)))
    lower_bounds = jnp.cumsum(jax.nn.softmax(hgrn_lb_logits.astype(F32), axis=0), axis=0)[0:1]
    zeros_state = jnp.zeros((nb, heads, HEAD_DIM, HEAD_DIM), F32)
    sg0 = jnp.concatenate([zeros_state, state_gdn[0]], axis=0)
    sh0 = jnp.concatenate([zeros_state, state_hgrn[0]], axis=0)
    cv0 = jnp.pad(jnp.concatenate([jnp.zeros((nb, CONV_W - 1, 3 * w), F32), cache_gdn_conv[0]], axis=0),
                  ((0, 0), (8 - (CONV_W - 1), 0), (0, 0)))

    xp = x_prompt.reshape(t_p, d)
    xs_tok = x_sample.reshape(db * dseq, d)

    mixed, sg, sh, cv = _front(xp, xs_tok, ln1_g, w_main, w_small, conv_w8, ad, gdn_norm_g,
                               lower_bounds, hgrn_norm_g, sg0, sh0, cv0, heads=heads,
                               n_prompt_seq=nb, n_prompt_chunks=seq // CHUNK,
                               n_sample_chunks=dseq // CHUNK)
    x1, hp, logits_t = _outproj(xp, xs_tok, mixed, w_out[0].astype(BF16), ln2_g,
                                w_router[0].T, b_router[0][:, None])

    e_idx, gates, ranks, counts = _route(logits_t)
    tm_e = 512 if t_all * TOP_K >= 512 * n_exp else 128
    cnt = counts[:, 0].astype(I32)
    padded = (cnt + tm_e - 1) // tm_e * tm_e
    seg_end = jnp.cumsum(padded)
    seg_start = seg_end - padded
    n_tiles = (t_all * TOP_K + n_exp * (tm_e - 1)) // tm_e
    onehot = e_idx[:TOP_K, :, None] == jnp.arange(n_exp, dtype=I32)
    dest = jnp.sum(jnp.where(onehot, seg_start, 0), axis=-1) + ranks[:TOP_K]
    tb = _pick(math.gcd(t_p, t_all - t_p), (256, 128))
    dest_g = dest.reshape(TOP_K, t_all // tb, tb // ROW_GROUP, ROW_GROUP).transpose(1, 2, 3, 0)
    dest_g = dest_g.reshape(t_all // tb, tb // ROW_GROUP, ROW_GROUP * TOP_K)
    tile_start = jnp.arange(n_tiles, dtype=I32) * tm_e
    tile_expert = jnp.minimum(jnp.sum(seg_end[None, :] <= tile_start[:, None], axis=1),
                              n_exp - 1).astype(I32)
    n_used = (seg_end[-1:] // tm_e).astype(I32)

    zrows = 64
    tail = jnp.stack([seg_end[-1], (n_tiles * tm_e - seg_end[-1]) // zrows]).astype(I32)
    xs = _dispatch(seg_start + cnt, padded - cnt, tail, dest_g, hp, n_tiles * tm_e, zrows)
    ys = _experts(tile_expert, n_used, xs, w_gate_up[0].astype(BF16), b_gate_up[0][:, None, :],
                  w_down[0].astype(BF16), b_down[0][:, None, :], tm=tm_e)
    gates_tk = gates[:TOP_K].T
    gf = ln_f_g[None, :]
    y_prompt = _combine(dest_g, x1, gates_tk, gf, ys, 0, t_p // tb).reshape(nb, seq, d)
    y_sample = _combine(dest_g, x1, gates_tk, gf, ys, t_p // tb, (t_all - t_p) // tb).reshape(db, dseq, d)
    cache = cv[:, 8 - (CONV_W - 1):, :]
    return (y_prompt, y_sample,
            sg[None, :nb], cache[None, :nb], sh[None, :nb],
            sg[None, nb:], cache[None, nb:], sh[None, nb:])
```
